```python
import jax, jax.numpy as jnp
from jax import lax
import numpy as np

D_MODEL = 2048
BATCH = 4
SEQ = 2048
DEPTH = 4
DEC_BATCH = 128
DEC_SEQ = 1
PAST_LEN = 16384
PAGE_SIZE = 128

RET_HEADS = 8
RET_HEAD_DIM = 128
RET_WIDTH = RET_HEADS * RET_HEAD_DIM
CONV_CH = D_MODEL - RET_WIDTH
MIX_WIDTH = RET_WIDTH + CONV_CH
CONV_K = 31
P_DIM = 256
RET_CHUNK = 128
ROPE_BASE = 10000.0
EPS = 1e-6
IN_COLS = 4 * RET_WIDTH + 3 * CONV_CH
SPLITS = (RET_WIDTH, 2 * RET_WIDTH, 3 * RET_WIDTH, 4 * RET_WIDTH,
          4 * RET_WIDTH + CONV_CH, 4 * RET_WIDTH + 2 * CONV_CH)

kernel_name = "hymba_retention_conformer_step"


def _rms_norm(x, g):
    xf = x.astype(jnp.float32)
    y = xf * lax.rsqrt(jnp.mean(xf * xf, axis=-1, keepdims=True) + EPS)
    return (y * g.astype(jnp.float32)).astype(x.dtype)


def _layer_norm(x, g, b):
    xf = x.astype(jnp.float32)
    mu = jnp.mean(xf, axis=-1, keepdims=True)
    var = jnp.mean(jnp.square(xf - mu), axis=-1, keepdims=True)
    y = (xf - mu) * lax.rsqrt(var + EPS)
    return (y * g.astype(jnp.float32) + b.astype(jnp.float32)).astype(x.dtype)


def _rope(x, pos):
    half = RET_HEAD_DIM // 2
    inv = ROPE_BASE ** (-jnp.arange(half, dtype=jnp.float32) / half)
    ang = pos.astype(jnp.float32)[:, None] * inv[None, :]
    cos = jnp.cos(ang)[None, :, None, :]
    sin = jnp.sin(ang)[None, :, None, :]
    xf = x.astype(jnp.float32)
    x1, x2 = xf[..., :half], xf[..., half:]
    return jnp.concatenate([x1 * cos - x2 * sin, x1 * sin + x2 * cos], axis=-1)


def _log_gamma():
    return jnp.log1p(-jnp.exp2(-5.0 - jnp.arange(RET_HEADS, dtype=jnp.float32)))


def _retention_chunk(q, k, v, S, log_g):
    L = q.shape[2]
    idx = jnp.arange(L, dtype=jnp.float32)
    rel = idx[:, None] - idx[None, :]
    decay = jnp.where(rel >= 0, jnp.exp(log_g[:, None, None] * jnp.maximum(rel, 0.0)), 0.0)
    scores = jnp.einsum('bhid,bhjd->bhij', q, k) * decay[None]
    o_intra = jnp.einsum('bhij,bhjv->bhiv', scores, v)
    cross_decay = jnp.exp(log_g[:, None] * (idx[None, :] + 1.0))
    o_cross = jnp.einsum('bhid,bhdv->bhiv', q, S) * cross_decay[None, :, :, None]
    k_decay = jnp.exp(log_g[:, None] * (L - 1.0 - idx[None, :]))
    S_new = (jnp.exp(log_g * L)[None, :, None, None] * S
             + jnp.einsum('bhjd,bhjv->bhdv', k * k_decay[None, :, :, None], v))
    return o_intra + o_cross, S_new


def _retention(q, k, v, S0):
    B, H, L, _ = q.shape
    chunk = RET_CHUNK if L % RET_CHUNK == 0 else L
    nc = L // chunk
    log_g = _log_gamma()

    def blocks(t):
        return t.reshape(B, H, nc, chunk, t.shape[-1]).transpose(2, 0, 1, 3, 4)

    def step(S, qkv):
        qc, kc, vc = qkv
        o, S = _retention_chunk(qc, kc, vc, S, log_g)
        return S, o

    S, o = lax.scan(step, S0, (blocks(q), blocks(k), blocks(v)))
    o = o.transpose(1, 2, 0, 3, 4).reshape(B, H, L, RET_HEAD_DIM)
    return o, S


def _mixer(h, S0, buf0, pos, w_in, w_out, g_ret, w_dw, b_dw, g_cn, b_cn, w_pw2, b_pw2):
    Bsz, L, _ = h.shape
    z = h @ w_in
    q, k, v, g_r, a, b, g_c = jnp.split(z, SPLITS, axis=-1)
    heads = lambda t: t.reshape(Bsz, L, RET_HEADS, RET_HEAD_DIM)
    q = _rope(heads(q), pos)
    k = _rope(heads(k), pos) * (RET_HEAD_DIM ** -0.5)
    vf = heads(v).astype(jnp.float32)
    tr = lambda t: t.transpose(0, 2, 1, 3)
    o, S = _retention(tr(q), tr(k), tr(vf), S0.astype(jnp.float32))
    o = tr(o)
    mu = jnp.mean(o, axis=-1, keepdims=True)
    var = jnp.mean(jnp.square(o - mu), axis=-1, keepdims=True)
    o = ((o - mu) * lax.rsqrt(var + EPS)).reshape(Bsz, L, RET_WIDTH) * g_ret.astype(jnp.float32)
    ret_out = o.astype(h.dtype) * jax.nn.silu(g_r)
    u = a * jax.nn.sigmoid(b)
    full = jnp.concatenate([buf0.astype(u.dtype), u], axis=1)
    c = lax.conv_general_dilated(full, w_dw[:, None, :].astype(u.dtype), (1,), 'VALID',
                                 dimension_numbers=('NWC', 'WIO', 'NWC'),
                                 feature_group_count=CONV_CH) + b_dw
    c = jax.nn.silu(_layer_norm(c, g_cn, b_cn))
    c = c @ w_pw2 + b_pw2
    conv_out = c * jax.nn.silu(g_c)
    m = jnp.concatenate([ret_out, conv_out], axis=-1) @ w_out
    return m, S.astype(h.dtype), full[:, -(CONV_K - 1):]


def _trunk(x, p, S_all, buf_all, pos, w_in, w_out, g_ret, w_dw, b_dw, g_cn, b_cn,
           w_pw2, b_pw2, g_pre, g_post, w_ple, g_ple, w_ple_gate):
    new_S, new_buf = [], []
    for i in range(DEPTH):
        hn = _rms_norm(x, g_pre[i])
        m, S, buf = _mixer(hn, S_all[i], buf_all[i], pos, w_in[i], w_out[i], g_ret[i],
                           w_dw[i], b_dw[i], g_cn[i], b_cn[i], w_pw2[i], b_pw2[i])
        h = x + _rms_norm(m, g_post[i])
        e = _rms_norm(p[i] @ w_ple[i], g_ple[i])
        x = h + jax.nn.sigmoid(h @ w_ple_gate[i]) * e
        new_S.append(S)
        new_buf.append(buf)
    return x, jnp.stack(new_S), jnp.stack(new_buf)


def setup_inputs(seed: int = 0) -> dict:
    key = jax.random.key(seed)
    ks = jax.random.split(key, 24)
    f32 = jnp.float32
    nrm = lambda k, shape, s: jax.random.normal(k, shape, f32) * s
    gain = lambda k, shape: 1.0 + 0.02 * jax.random.normal(k, shape, f32)
    return {
        "x_prompt": nrm(ks[0], (BATCH, SEQ, D_MODEL), 1.0),
        "x_sample": nrm(ks[1], (DEC_BATCH, DEC_SEQ, D_MODEL), 1.0),
        "state_ret": nrm(ks[2], (DEPTH, DEC_BATCH, RET_HEADS, RET_HEAD_DIM, RET_HEAD_DIM), 0.05),
        "state_conv": nrm(ks[3], (DEPTH, DEC_BATCH, CONV_K - 1, CONV_CH), 0.5),
        "p_prompt": nrm(ks[4], (DEPTH, BATCH, SEQ, P_DIM), 1.0),
        "p_sample": nrm(ks[5], (DEPTH, DEC_BATCH, DEC_SEQ, P_DIM), 1.0),
        "w_in": nrm(ks[6], (DEPTH, D_MODEL, IN_COLS), D_MODEL ** -0.5),
        "w_out": nrm(ks[7], (DEPTH, MIX_WIDTH, D_MODEL), MIX_WIDTH ** -0.5),
        "g_ret": gain(ks[8], (DEPTH, RET_WIDTH)),
        "w_dw": nrm(ks[9], (DEPTH, CONV_K, CONV_CH), CONV_K ** -0.5),
        "b_dw": nrm(ks[10], (DEPTH, CONV_CH), 0.01),
        "g_cn": gain(ks[11], (DEPTH, CONV_CH)),
        "b_cn": nrm(ks[12], (DEPTH, CONV_CH), 0.01),
        "w_pw2": nrm(ks[13], (DEPTH, CONV_CH, CONV_CH), CONV_CH ** -0.5),
        "b_pw2": nrm(ks[14], (DEPTH, CONV_CH), 0.01),
        "g_pre": gain(ks[15], (DEPTH, D_MODEL)),
        "g_post": gain(ks[16], (DEPTH, D_MODEL)),
        "w_ple": nrm(ks[17], (DEPTH, P_DIM, D_MODEL), P_DIM ** -0.5),
        "g_ple": gain(ks[18], (DEPTH, D_MODEL)),
        "w_ple_gate": nrm(ks[19], (DEPTH, D_MODEL, D_MODEL), D_MODEL ** -0.5),
    }


def reference(x_prompt, x_sample, state_ret, state_conv, p_prompt, p_sample, w_in, w_out,
              g_ret, w_dw, b_dw, g_cn, b_cn, w_pw2, b_pw2, g_pre, g_post, w_ple, g_ple,
              w_ple_gate):
    weights = (w_in, w_out, g_ret, w_dw, b_dw, g_cn, b_cn, w_pw2, b_pw2,
               g_pre, g_post, w_ple, g_ple, w_ple_gate)
    Bp, Lp, _ = x_prompt.shape
    S0_p = jnp.zeros((DEPTH, Bp, RET_HEADS, RET_HEAD_DIM, RET_HEAD_DIM), x_prompt.dtype)
    buf0_p = jnp.zeros((DEPTH, Bp, CONV_K - 1, CONV_CH), x_prompt.dtype)
    pos_p = jnp.arange(Lp, dtype=jnp.int32)
    y_prompt, state_ret_prompt, state_conv_prompt = _trunk(
        x_prompt, p_prompt, S0_p, buf0_p, pos_p, *weights)
    Ls = x_sample.shape[1]
    pos_s = PAST_LEN + jnp.arange(Ls, dtype=jnp.int32)
    y_sample, state_ret_sample, state_conv_sample = _trunk(
        x_sample, p_sample, state_ret, state_conv, pos_s, *weights)
    return (y_prompt, y_sample, state_ret_prompt, state_conv_prompt,
            state_ret_sample, state_conv_sample)
```

```python
import functools

import jax
import jax.numpy as jnp
from jax import lax
from jax.experimental import pallas as pl
from jax.experimental.pallas import tpu as pltpu

F32 = jnp.float32
BF16 = jnp.bfloat16

RET_HEADS = 8
HEAD_DIM = 128
RET_WIDTH = RET_HEADS * HEAD_DIM
CONV_K = 31
ROPE_BASE = 10000.0
EPS = 1e-6
PAST_LEN = 16384

VMEM_LIMIT_BYTES = 56 * 1024 * 1024
COL_TILE = 1024
RET_CHUNK = 256
CONV_ROW_BLOCK = 64
HIST_ROWS = 32


def _params(sem):
    return pltpu.CompilerParams(dimension_semantics=sem,
                                vmem_limit_bytes=VMEM_LIMIT_BYTES)


def _silu(x):
    return x * jax.nn.sigmoid(x)


def _rms(x, g):
    return x * lax.rsqrt(jnp.mean(x * x, axis=-1, keepdims=True) + EPS) * g


def _in_proj_kernel(x_ref, gpre_ref, w_ref, cos_ref, sin_ref,
                    zr_ref, u_ref, gc_ref, hn_ref, a_ref):
    j = pl.program_id(1)

    @pl.when(j == 0)
    def _():
        hn_ref[...] = _rms(x_ref[...], gpre_ref[...]).astype(BF16)

    acc = jnp.dot(hn_ref[...], w_ref[...], preferred_element_type=F32)

    @pl.when(j <= 1)
    def _():
        scale = jnp.where(j == 1, HEAD_DIM ** -0.5, 1.0).astype(F32)
        c = cos_ref[...]
        s = sin_ref[...]
        for h in range(RET_HEADS):
            xh = acc[:, h * HEAD_DIM:(h + 1) * HEAD_DIM]
            rot = pltpu.roll(xh, HEAD_DIM // 2, axis=1)
            zr_ref[:, h * HEAD_DIM:(h + 1) * HEAD_DIM] = (
                (xh * c + rot * s) * scale).astype(BF16)

    @pl.when(j == 2)
    def _():
        zr_ref[...] = acc.astype(BF16)

    @pl.when(j == 3)
    def _():
        zr_ref[...] = _silu(acc).astype(BF16)

    @pl.when(j == 4)
    def _():
        a_ref[...] = acc

    @pl.when(j == 5)
    def _():
        u_ref[...] = a_ref[...] * jax.nn.sigmoid(acc)

    @pl.when(j == 6)
    def _():
        gc_ref[...] = _silu(acc).astype(BF16)


def _in_proj(x, g_pre, w_in, cos_t, sin_t, *, bm, pos_blocks):
    m, d = x.shape
    n_groups = w_in.shape[1] // COL_TILE
    assert n_groups == 7 and m % bm == 0
    return pl.pallas_call(
        _in_proj_kernel,
        grid=(m // bm, n_groups),
        in_specs=[
            pl.BlockSpec((bm, d), lambda i, j: (i, 0)),
            pl.BlockSpec((1, d), lambda i, j: (0, 0)),
            pl.BlockSpec((d, COL_TILE), lambda i, j: (0, j)),
            pl.BlockSpec((bm, HEAD_DIM), lambda i, j: (i % pos_blocks, 0)),
            pl.BlockSpec((bm, HEAD_DIM), lambda i, j: (i % pos_blocks, 0)),
        ],
        out_specs=[
            pl.BlockSpec((bm, COL_TILE), lambda i, j: (i, jnp.minimum(j, 3))),
            pl.BlockSpec((bm, COL_TILE), lambda i, j: (i, 0)),
            pl.BlockSpec((bm, COL_TILE), lambda i, j: (i, 0)),
        ],
        out_shape=[
            jax.ShapeDtypeStruct((m, 4 * COL_TILE), BF16),
            jax.ShapeDtypeStruct((m, COL_TILE), F32),
            jax.ShapeDtypeStruct((m, COL_TILE), BF16),
        ],
        scratch_shapes=[pltpu.VMEM((bm, d), BF16), pltpu.VMEM((bm, COL_TILE), F32)],
        compiler_params=_params(("arbitrary", "arbitrary")),
        name="in_proj",
    )(x, g_pre, w_in, cos_t, sin_t)


def _group_norm_gate(o, g_ret, gate):
    mu = jnp.mean(o, axis=-1, keepdims=True)
    d = o - mu
    var = jnp.mean(d * d, axis=-1, keepdims=True)
    return (d * lax.rsqrt(var + EPS)) * g_ret * gate


def _retention_kernel(q_ref, k_ref, v_ref, gr_ref, gret_ref, lg_ref, o_ref, s_ref):
    c = RET_CHUNK
    n_chunks = q_ref.shape[0] // c
    lg = lg_ref[0]
    row = lax.broadcasted_iota(jnp.int32, (c, c), 0)
    col = lax.broadcasted_iota(jnp.int32, (c, c), 1)
    rel = (row - col).astype(F32)
    decay = jnp.where(rel >= 0, jnp.exp(lg * jnp.maximum(rel, 0.0)), 0.0)
    idx = lax.broadcasted_iota(jnp.int32, (c, 1), 0).astype(F32)
    lg1 = lg[:, :1]
    cross = jnp.exp(lg1 * (idx + 1.0))
    k_decay = jnp.exp(lg1 * (c - 1.0 - idx))
    g_chunk = jnp.exp(lg1 * float(c))
    g_ret = gret_ref[0]

    def body(ci, s):
        rows = pl.ds(pl.multiple_of(ci * c, c), c)
        q = q_ref[rows, :]
        k = k_ref[rows, :]
        v = v_ref[rows, :]
        scores = lax.dot_general(q, k, (((1,), (1,)), ((), ())),
                                 preferred_element_type=F32) * decay
        o = jnp.dot(scores.astype(BF16), v, preferred_element_type=F32)
        o = o + jnp.dot(q, s.astype(BF16), preferred_element_type=F32) * cross
        kd = (k.astype(F32) * k_decay).astype(BF16)
        s_new = g_chunk * s + lax.dot_general(kd, v, (((0,), (0,)), ((), ())),
                                              preferred_element_type=F32)
        o_ref[rows, :] = _group_norm_gate(
            o, g_ret, gr_ref[rows, :].astype(F32)).astype(BF16)
        return s_new

    s = lax.fori_loop(0, n_chunks, body, jnp.zeros((HEAD_DIM, HEAD_DIM), F32))
    s_ref[0, 0] = s


def _retention(zr, g_ret3, lg_tab, *, batch, seq):
    m = zr.shape[0]
    blk = lambda off: pl.BlockSpec((seq, HEAD_DIM), lambda b, h: (b, off + h))
    return pl.pallas_call(
        _retention_kernel,
        grid=(batch, RET_HEADS),
        in_specs=[
            blk(0), blk(RET_HEADS), blk(2 * RET_HEADS), blk(3 * RET_HEADS),
            pl.BlockSpec((1, 1, HEAD_DIM), lambda b, h: (h, 0, 0)),
            pl.BlockSpec((1, 1, RET_CHUNK), lambda b, h: (h, 0, 0)),
        ],
        out_specs=[
            pl.BlockSpec((seq, HEAD_DIM), lambda b, h: (b, h)),
            pl.BlockSpec((1, 1, HEAD_DIM, HEAD_DIM), lambda b, h: (b, h, 0, 0)),
        ],
        out_shape=[
            jax.ShapeDtypeStruct((m, RET_WIDTH), BF16),
            jax.ShapeDtypeStruct((batch, RET_HEADS, HEAD_DIM, HEAD_DIM), F32),
        ],
        compiler_params=_params(("arbitrary", "arbitrary")),
        name="retention",
    )(zr, zr, zr, zr, g_ret3, lg_tab)


def _retention_step_kernel(zr_ref, s_ref, gret_ref, gam_ref, o_ref, s_out_ref, oc_ref,
                           *, bb):
    i = pl.program_id(0)
    n_b = zr_ref.shape[0]
    rows = pl.ds(pl.multiple_of(i * bb, bb), bb)
    shift = (n_b - i * bb) % n_b
    for h in range(RET_HEADS):
        hs = lambda g: slice((g * RET_HEADS + h) * HEAD_DIM,
                             (g * RET_HEADS + h + 1) * HEAD_DIM)
        q_t = pltpu.roll(zr_ref[:, hs(0)].astype(F32).T, shift, axis=1)
        k_t = pltpu.roll(zr_ref[:, hs(1)].astype(F32).T, shift, axis=1)
        q = zr_ref[rows, hs(0)].astype(F32)
        k = zr_ref[rows, hs(1)].astype(F32)
        v = zr_ref[rows, hs(2)].astype(F32)
        gate = zr_ref[rows, hs(3)].astype(F32)
        gam = gam_ref[h]
        for b in range(bb):
            s = s_ref[b, h]
            oc_ref[b:b + 1, :] = jnp.sum(q_t[:, b:b + 1] * s, axis=0, keepdims=True)
            s_out_ref[b, h] = gam * s + k_t[:, b:b + 1] * v[b:b + 1, :]
        o = jnp.sum(q * k, axis=-1, keepdims=True) * v + oc_ref[...] * gam
        o_ref[:, hs(0)] = _group_norm_gate(o, gret_ref[h], gate).astype(BF16)


def _retention_step(zr, state, g_ret3, gam_tab, *, bb):
    n_b = zr.shape[0]
    return pl.pallas_call(
        functools.partial(_retention_step_kernel, bb=bb),
        grid=(n_b // bb,),
        in_specs=[
            pl.BlockSpec(zr.shape, lambda i: (0, 0)),
            pl.BlockSpec((bb, RET_HEADS, HEAD_DIM, HEAD_DIM), lambda i: (i, 0, 0, 0)),
            pl.BlockSpec((RET_HEADS, 1, HEAD_DIM), lambda i: (0, 0, 0)),
            pl.BlockSpec((RET_HEADS, 1, HEAD_DIM), lambda i: (0, 0, 0)),
        ],
        out_specs=[
            pl.BlockSpec((bb, RET_WIDTH), lambda i: (i, 0)),
            pl.BlockSpec((bb, RET_HEADS, HEAD_DIM, HEAD_DIM), lambda i: (i, 0, 0, 0)),
        ],
        out_shape=[
            jax.ShapeDtypeStruct((n_b, RET_WIDTH), BF16),
            jax.ShapeDtypeStruct(state.shape, F32),
        ],
        scratch_shapes=[pltpu.VMEM((bb, HEAD_DIM), F32)],
        compiler_params=_params(("arbitrary",)),
        name="retention_step",
    )(zr, state, g_ret3, gam_tab)


def _conv_tail(c, gcn_ref, bcn_ref, wpw_ref, bpw_ref, gate):
    mu = jnp.mean(c, axis=-1, keepdims=True)
    d = c - mu
    var = jnp.mean(d * d, axis=-1, keepdims=True)
    y = _silu(d * lax.rsqrt(var + EPS) * gcn_ref[...] + bcn_ref[...])
    y = jnp.dot(y.astype(BF16), wpw_ref[...], preferred_element_type=F32) + bpw_ref[...]
    return (y * gate).astype(BF16)


def _conv_kernel(u_ref, gc_ref, wdw_ref, bdw_ref, gcn_ref, bcn_ref, wpw_ref, bpw_ref,
                 o_ref, st_ref, full_ref, c_ref):
    t = pl.program_id(1)
    tt, ch = u_ref.shape
    pad = HIST_ROWS - (CONV_K - 1)

    @pl.when(t == 0)
    def _():
        full_ref[0:HIST_ROWS, :] = jnp.zeros((HIST_ROWS, ch), F32)

    full_ref[HIST_ROWS:HIST_ROWS + tt, :] = u_ref[...]

    for r0 in range(0, tt, CONV_ROW_BLOCK):
        for cb in range(ch // 128):
            cs = slice(cb * 128, (cb + 1) * 128)
            acc = full_ref[r0 + pad:r0 + pad + CONV_ROW_BLOCK, cs] * wdw_ref[0:1, cs]
            for j in range(1, CONV_K):
                acc = acc + (full_ref[r0 + pad + j:r0 + pad + j + CONV_ROW_BLOCK, cs]
                             * wdw_ref[j:j + 1, cs])
            c_ref[r0:r0 + CONV_ROW_BLOCK, cs] = acc + bdw_ref[:, cs]

    o_ref[...] = _conv_tail(c_ref[...], gcn_ref, bcn_ref, wpw_ref, bpw_ref,
                            gc_ref[...].astype(F32))

    @pl.when(t == pl.num_programs(1) - 1)
    def _():
        st_ref[0] = full_ref[tt + pad:tt + HIST_ROWS, :]

    full_ref[0:HIST_ROWS, :] = full_ref[tt:tt + HIST_ROWS, :]


def _conv(u, gc, w_dw, b_dw, g_cn, b_cn, w_pw, b_pw, *, batch, seq, tt):
    m, ch = u.shape
    nt = seq // tt
    vec = pl.BlockSpec((1, ch), lambda b, t: (0, 0))
    return pl.pallas_call(
        _conv_kernel,
        grid=(batch, nt),
        in_specs=[
            pl.BlockSpec((tt, ch), lambda b, t: (b * nt + t, 0)),
            pl.BlockSpec((tt, ch), lambda b, t: (b * nt + t, 0)),
            pl.BlockSpec((CONV_K, ch), lambda b, t: (0, 0)),
            vec, vec, vec,
            pl.BlockSpec((ch, ch), lambda b, t: (0, 0)),
            vec,
        ],
        out_specs=[
            pl.BlockSpec((tt, ch), lambda b, t: (b * nt + t, 0)),
            pl.BlockSpec((1, CONV_K - 1, ch), lambda b, t: (b, 0, 0)),
        ],
        out_shape=[
            jax.ShapeDtypeStruct((m, ch), BF16),
            jax.ShapeDtypeStruct((batch, CONV_K - 1, ch), F32),
        ],
        scratch_shapes=[pltpu.VMEM((HIST_ROWS + tt, ch), F32), pltpu.VMEM((tt, ch), F32)],
        compiler_params=_params(("arbitrary", "arbitrary")),
        name="conv",
    )(u, gc, w_dw, b_dw, g_cn, b_cn, w_pw, b_pw)


def _conv_step_kernel(u_ref, gc_ref, st_ref, wdw_ref, bdw_ref, gcn_ref, bcn_ref, wpw_ref,
                      bpw_ref, o_ref, st_out_ref):
    bb = u_ref.shape[0]
    hist = CONV_K - 1
    u = u_ref[...]
    w = wdw_ref[...]
    c = (jnp.sum(st_ref[...] * w[None, :hist, :], axis=1)
         + u * w[hist:hist + 1, :] + bdw_ref[...])
    o_ref[...] = _conv_tail(c, gcn_ref, bcn_ref, wpw_ref, bpw_ref,
                            gc_ref[...].astype(F32))
    st_out_ref[:, 0:hist - 1, :] = st_ref[:, 1:hist, :]
    for b in range(bb):
        st_out_ref[b, hist - 1:hist, :] = u_ref[b:b + 1, :]


def _conv_step(u, gc, state, w_dw, b_dw, g_cn, b_cn, w_pw, b_pw, *, bb):
    n_b, ch = u.shape
    vec = pl.BlockSpec((1, ch), lambda i: (0, 0))
    st_spec = pl.BlockSpec((bb, CONV_K - 1, ch), lambda i: (i, 0, 0))
    return pl.pallas_call(
        _conv_step_kernel,
        grid=(n_b // bb,),
        in_specs=[
            pl.BlockSpec((bb, ch), lambda i: (i, 0)),
            pl.BlockSpec((bb, ch), lambda i: (i, 0)),
            st_spec,
            pl.BlockSpec((CONV_K, ch), lambda i: (0, 0)),
            vec, vec, vec,
            pl.BlockSpec((ch, ch), lambda i: (0, 0)),
            vec,
        ],
        out_specs=[pl.BlockSpec((bb, ch), lambda i: (i, 0)), st_spec],
        out_shape=[
            jax.ShapeDtypeStruct((n_b, ch), BF16),
            jax.ShapeDtypeStruct(state.shape, F32),
        ],
        compiler_params=_params(("arbitrary",)),
        name="conv_step",
    )(u, gc, state, w_dw, b_dw, g_cn, b_cn, w_pw, b_pw)


def _out_kernel(ret_ref, conv_ref, x_ref, p_ref, wo_ref, wg_ref, wp_ref, gpost_ref,
                gple_ref, y_ref):
    rw = ret_ref.shape[1]
    m = (jnp.dot(ret_ref[...], wo_ref[0:rw, :], preferred_element_type=F32)
         + jnp.dot(conv_ref[...], wo_ref[rw:, :], preferred_element_type=F32))
    h = x_ref[...] + _rms(m, gpost_ref[...])
    gate = jax.nn.sigmoid(
        jnp.dot(h.astype(BF16), wg_ref[...], preferred_element_type=F32))
    e = jnp.dot(p_ref[...].astype(BF16), wp_ref[...], preferred_element_type=F32)
    y_ref[...] = h + gate * _rms(e, gple_ref[...])


def _out_proj(ret, conv, x, p, w_out, w_gate, w_ple, g_post, g_ple, *, bm):
    m, d = x.shape
    resident = lambda shape: pl.BlockSpec(shape, lambda i: (0, 0),
                                          pipeline_mode=pl.Buffered(1))
    rows = lambda width: pl.BlockSpec((bm, width), lambda i: (i, 0))
    return pl.pallas_call(
        _out_kernel,
        grid=(m // bm,),
        in_specs=[
            rows(ret.shape[1]), rows(conv.shape[1]), rows(d), rows(p.shape[1]),
            resident(w_out.shape), resident(w_gate.shape), resident(w_ple.shape),
            resident((1, d)), resident((1, d)),
        ],
        out_specs=rows(d),
        out_shape=jax.ShapeDtypeStruct((m, d), F32),
        compiler_params=_params(("arbitrary",)),
        name="out_proj",
    )(ret, conv, x, p, w_out, w_gate, w_ple, g_post, g_ple)


def _rope_tables(pos):
    half = HEAD_DIM // 2
    inv = ROPE_BASE ** (-jnp.arange(half, dtype=F32) / half)
    ang = pos.astype(F32)[:, None] * inv[None, :]
    cos, sin = jnp.cos(ang), jnp.sin(ang)
    return jnp.concatenate([cos, cos], axis=-1), jnp.concatenate([-sin, sin], axis=-1)


def kernel(x_prompt, x_sample, state_ret, state_conv, p_prompt, p_sample, w_in, w_out,
           g_ret, w_dw, b_dw, g_cn, b_cn, w_pw2, b_pw2, g_pre, g_post, w_ple, g_ple,
           w_ple_gate):
    depth = w_in.shape[0]
    bp, lp, d = x_prompt.shape
    bs, ls, _ = x_sample.shape
    assert ls == 1 and lp % RET_CHUNK == 0

    w_in_b, w_out_b, w_pw_b = w_in.astype(BF16), w_out.astype(BF16), w_pw2.astype(BF16)
    w_ple_b, w_gate_b = w_ple.astype(BF16), w_ple_gate.astype(BF16)

    log_g = jnp.log1p(-jnp.exp2(-5.0 - jnp.arange(RET_HEADS, dtype=F32)))
    lg_tab = jnp.broadcast_to(log_g[:, None, None], (RET_HEADS, 1, RET_CHUNK))
    gam_tab = jnp.broadcast_to(jnp.exp(log_g)[:, None, None], (RET_HEADS, 1, HEAD_DIM))
    cos_p, sin_p = _rope_tables(jnp.arange(lp, dtype=jnp.int32))
    cos_s, sin_s = _rope_tables(jnp.full((bs,), PAST_LEN, dtype=jnp.int32))

    row = lambda a: a.reshape(1, -1)
    bm_p, tt = 512, 256
    xp = x_prompt.reshape(bp * lp, d)
    xs = x_sample.reshape(bs, d)
    ret_p, conv_p, ret_s, conv_s = [], [], [], []
    for i in range(depth):
        g_ret3 = g_ret[i].reshape(RET_HEADS, 1, HEAD_DIM)
        conv_w = (w_dw[i], row(b_dw[i]), row(g_cn[i]), row(b_cn[i]), w_pw_b[i],
                  row(b_pw2[i]))
        out_w = (w_out_b[i], w_gate_b[i], w_ple_b[i], row(g_post[i]), row(g_ple[i]))

        zr, u, gc = _in_proj(xp, row(g_pre[i]), w_in_b[i], cos_p, sin_p,
                             bm=bm_p, pos_blocks=lp // bm_p)
        r_out, s_new = _retention(zr, g_ret3, lg_tab, batch=bp, seq=lp)
        c_out, buf_new = _conv(u, gc, *conv_w, batch=bp, seq=lp, tt=tt)
        xp = _out_proj(r_out, c_out, xp, p_prompt[i].reshape(bp * lp, -1), *out_w, bm=bm_p)
        ret_p.append(s_new)
        conv_p.append(buf_new)

        zr, u, gc = _in_proj(xs, row(g_pre[i]), w_in_b[i], cos_s, sin_s,
                             bm=bs, pos_blocks=1)
        r_out, s_new = _retention_step(zr, state_ret[i], g_ret3, gam_tab, bb=16)
        c_out, buf_new = _conv_step(u, gc, state_conv[i], *conv_w, bb=16)
        xs = _out_proj(r_out, c_out, xs, p_sample[i].reshape(bs, -1), *out_w, bm=bs)
        ret_s.append(s_new)
        conv_s.append(buf_new)

    return (xp.reshape(bp, lp, d), xs.reshape(bs, ls, d), jnp.stack(ret_p),
            jnp.stack(conv_p), jnp.stack(ret_s), jnp.stack(conv_s))
```

```python
import functools

import jax
import jax.numpy as jnp
from jax import lax
from jax.experimental import pallas as pl
from jax.experimental.pallas import tpu as pltpu

F32 = jnp.float32
BF16 = jnp.bfloat16

RET_HEADS = 8
HEAD_DIM = 128
RET_WIDTH = RET_HEADS * HEAD_DIM
CONV_K = 31
ROPE_BASE = 10000.0
EPS = 1e-6
PAST_LEN = 16384

VMEM_LIMIT_BYTES = 58 * 1024 * 1024
LANES = 128
SUBLANES = 8
COL_TILE = 1024
RET_CHUNK = 256
CONV_ROW_BLOCK = 64
HIST_ROWS = 32


def _params(sem):
    return pltpu.CompilerParams(dimension_semantics=sem,
                                vmem_limit_bytes=VMEM_LIMIT_BYTES)


def _silu(x):
    return x * jax.nn.sigmoid(x)


def _rms(x, g):
    return x * lax.rsqrt(jnp.mean(x * x, axis=-1, keepdims=True) + EPS) * g


def _layer_spec(shape, li, n_grid, buffered=None):
    zeros = (0,) * len(shape)
    if n_grid == 1:
        index_map = lambda i: (li,) + zeros
    else:
        index_map = lambda i, j: (li,) + zeros
    if buffered is None:
        return pl.BlockSpec((None,) + tuple(shape), index_map)
    return pl.BlockSpec((None,) + tuple(shape), index_map,
                        pipeline_mode=pl.Buffered(buffered))


def _with_alias_slot(body, n_in):
    def wrapped(*refs):
        return body(*refs[:n_in], *refs[n_in + 1:])
    return wrapped


def _alias_args(body, n_in, prev, out_idx):
    if prev is None:
        return body, [], [], {}
    return (_with_alias_slot(body, n_in), [prev],
            [pl.BlockSpec(memory_space=pl.ANY)], {n_in: out_idx})


def _in_proj_kernel(x_ref, gpre_ref, w_ref, cos_ref, sin_ref, zr_ref, u_ref, gc_ref):
    hn = _rms(x_ref[...], gpre_ref[...]).astype(BF16)

    def proj(g):
        return jnp.dot(hn, w_ref[:, g * COL_TILE:(g + 1) * COL_TILE],
                       preferred_element_type=F32)

    def rope(z, scale, col0):
        c = cos_ref[...]
        s = sin_ref[...]
        for h in range(RET_HEADS):
            xh = z[:, h * HEAD_DIM:(h + 1) * HEAD_DIM]
            rot = pltpu.roll(xh, HEAD_DIM // 2, axis=1)
            r = xh * c + rot * s
            if scale is not None:
                r = r * scale
            zr_ref[:, col0 + h * HEAD_DIM:col0 + (h + 1) * HEAD_DIM] = r.astype(BF16)

    rope(proj(0), None, 0)
    rope(proj(1), HEAD_DIM ** -0.5, COL_TILE)
    zr_ref[:, 2 * COL_TILE:3 * COL_TILE] = proj(2).astype(BF16)
    zr_ref[:, 3 * COL_TILE:4 * COL_TILE] = _silu(proj(3)).astype(BF16)
    u_ref[...] = proj(4) * jax.nn.sigmoid(proj(5))
    gc_ref[...] = _silu(proj(6)).astype(BF16)


def _in_proj(x, g_pre, w_in, cos_t, sin_t, li, *, bm, pos_blocks):
    m, d = x.shape
    n_cols = w_in.shape[2]
    assert n_cols == 7 * COL_TILE and m % bm == 0
    rows = lambda width: pl.BlockSpec((bm, width), lambda i: (i, 0))
    pos = pl.BlockSpec((bm, HEAD_DIM), lambda i: (i % pos_blocks, 0))
    return pl.pallas_call(
        _in_proj_kernel,
        grid=(m // bm,),
        in_specs=[rows(d), _layer_spec((1, d), li, 1), _layer_spec((d, n_cols), li, 1, 1),
                  pos, pos],
        out_specs=[rows(4 * COL_TILE), rows(COL_TILE), rows(COL_TILE)],
        out_shape=[
            jax.ShapeDtypeStruct((m, 4 * COL_TILE), BF16),
            jax.ShapeDtypeStruct((m, COL_TILE), F32),
            jax.ShapeDtypeStruct((m, COL_TILE), BF16),
        ],
        compiler_params=_params(("arbitrary",)),
        name="in_proj",
    )(x, g_pre, w_in, cos_t, sin_t)


def _group_norm_gate(o, g_ret, gate):
    mu = jnp.mean(o, axis=-1, keepdims=True)
    d = o - mu
    var = jnp.mean(d * d, axis=-1, keepdims=True)
    return (d * lax.rsqrt(var + EPS)) * g_ret * gate


def _retention_kernel(q_ref, k_ref, v_ref, gr_ref, gret_ref, lg_ref, o_ref, s_out_ref,
                      s_ref, decay_ref, cross_ref, kdec_ref):
    c = RET_CHUNK
    b = pl.program_id(0)
    t = pl.program_id(1)

    @pl.when((b == 0) & (t == 0))
    def _():
        row = lax.broadcasted_iota(jnp.int32, (c, c), 0)
        col = lax.broadcasted_iota(jnp.int32, (c, c), 1)
        rel = (row - col).astype(F32)
        idx = lax.broadcasted_iota(jnp.int32, (c, LANES), 0).astype(F32)
        for h in range(RET_HEADS):
            lg = lg_ref[h]
            decay_ref[h] = jnp.where(rel >= 0, jnp.exp(lg * jnp.maximum(rel, 0.0)), 0.0)
            cross_ref[h] = jnp.exp(lg[:, :LANES] * (idx + 1.0))
            kdec_ref[h] = jnp.exp(lg[:, :LANES] * (c - 1.0 - idx))

    @pl.when(t == 0)
    def _():
        s_ref[...] = jnp.zeros(s_ref.shape, F32)

    for ci in range(q_ref.shape[0] // c):
        rows = slice(ci * c, (ci + 1) * c)
        for h in range(RET_HEADS):
            cols = slice(h * HEAD_DIM, (h + 1) * HEAD_DIM)
            q = q_ref[rows, cols]
            k = k_ref[rows, cols]
            v = v_ref[rows, cols]
            s = s_ref[h]
            g_chunk = jnp.exp(lg_ref[h][:, :LANES] * float(c))
            scores = lax.dot_general(q, k, (((1,), (1,)), ((), ())),
                                     preferred_element_type=F32) * decay_ref[h]
            o = jnp.dot(scores.astype(BF16), v, preferred_element_type=F32)
            o = o + jnp.dot(q, s.astype(BF16), preferred_element_type=F32) * cross_ref[h]
            kd = (k.astype(F32) * kdec_ref[h]).astype(BF16)
            s_ref[h] = g_chunk * s + lax.dot_general(
                kd, v, (((0,), (0,)), ((), ())), preferred_element_type=F32)
            o_ref[rows, cols] = _group_norm_gate(
                o, gret_ref[h], gr_ref[rows, cols].astype(F32)).astype(BF16)

    @pl.when(t == pl.num_programs(1) - 1)
    def _():
        s_out_ref[...] = s_ref[...]


def _retention(zr, g_ret, lg_tab, s_prev, li, depth, *, batch, seq, tq):
    m = zr.shape[0]
    nt = seq // tq
    blk = lambda g: pl.BlockSpec((tq, RET_WIDTH), lambda b, t: (b * nt + t, g))
    in_arrays = [zr, zr, zr, zr, g_ret, lg_tab]
    in_specs = [blk(0), blk(1), blk(2), blk(3),
                _layer_spec((RET_HEADS, 1, HEAD_DIM), li, 2),
                pl.BlockSpec(lg_tab.shape, lambda b, t: (0, 0, 0))]
    body, extra, extra_specs, aliases = _alias_args(
        _retention_kernel, len(in_arrays), s_prev, 1)
    return pl.pallas_call(
        body,
        grid=(batch, nt),
        in_specs=in_specs + extra_specs,
        out_specs=[
            pl.BlockSpec((tq, RET_WIDTH), lambda b, t: (b * nt + t, 0)),
            pl.BlockSpec((None, None, RET_HEADS, HEAD_DIM, HEAD_DIM),
                         lambda b, t: (li, b, 0, 0, 0)),
        ],
        out_shape=[
            jax.ShapeDtypeStruct((m, RET_WIDTH), BF16),
            jax.ShapeDtypeStruct((depth, batch, RET_HEADS, HEAD_DIM, HEAD_DIM), F32),
        ],
        scratch_shapes=[
            pltpu.VMEM((RET_HEADS, HEAD_DIM, HEAD_DIM), F32),
            pltpu.VMEM((RET_HEADS, RET_CHUNK, RET_CHUNK), F32),
            pltpu.VMEM((RET_HEADS, RET_CHUNK, LANES), F32),
            pltpu.VMEM((RET_HEADS, RET_CHUNK, LANES), F32),
        ],
        input_output_aliases=aliases,
        compiler_params=_params(("arbitrary", "arbitrary")),
        name="retention",
    )(*in_arrays, *extra)


def _retention_step_kernel(zr_ref, s_ref, gret_ref, gam_ref, o_ref, s_out_ref, oc_ref,
                           *, bb):
    i = pl.program_id(0)
    n_b = zr_ref.shape[0]
    rows = pl.ds(pl.multiple_of(i * bb, bb), bb)
    shift = (n_b - i * bb) % n_b
    for h in range(RET_HEADS):
        hs = lambda g: slice((g * RET_HEADS + h) * HEAD_DIM,
                             (g * RET_HEADS + h + 1) * HEAD_DIM)
        q_t = pltpu.roll(zr_ref[:, hs(0)].astype(F32).T, shift, axis=1)
        k_t = pltpu.roll(zr_ref[:, hs(1)].astype(F32).T, shift, axis=1)
        q = zr_ref[rows, hs(0)].astype(F32)
        k = zr_ref[rows, hs(1)].astype(F32)
        v = zr_ref[rows, hs(2)].astype(F32)
        gate = zr_ref[rows, hs(3)].astype(F32)
        gam = gam_ref[h]
        for b in range(bb):
            s = s_ref[b, h]
            oc_ref[b:b + 1, :] = jnp.sum(q_t[:, b:b + 1] * s, axis=0, keepdims=True)
            s_out_ref[b, h] = gam * s + k_t[:, b:b + 1] * v[b:b + 1, :]
        o = jnp.sum(q * k, axis=-1, keepdims=True) * v + oc_ref[...] * gam
        o_ref[:, hs(0)] = _group_norm_gate(o, gret_ref[h], gate).astype(BF16)


def _retention_step(zr, state, g_ret, gam_tab, s_prev, li, *, bb):
    n_b = zr.shape[0]
    st_spec = pl.BlockSpec((None, bb, RET_HEADS, HEAD_DIM, HEAD_DIM),
                           lambda i: (li, i, 0, 0, 0))
    in_arrays = [zr, state, g_ret, gam_tab]
    in_specs = [pl.BlockSpec(zr.shape, lambda i: (0, 0)), st_spec,
                _layer_spec((RET_HEADS, 1, HEAD_DIM), li, 1),
                pl.BlockSpec(gam_tab.shape, lambda i: (0, 0, 0))]
    body, extra, extra_specs, aliases = _alias_args(
        functools.partial(_retention_step_kernel, bb=bb), len(in_arrays), s_prev, 1)
    return pl.pallas_call(
        body,
        grid=(n_b // bb,),
        in_specs=in_specs + extra_specs,
        out_specs=[pl.BlockSpec((bb, RET_WIDTH), lambda i: (i, 0)), st_spec],
        out_shape=[
            jax.ShapeDtypeStruct((n_b, RET_WIDTH), BF16),
            jax.ShapeDtypeStruct(state.shape, F32),
        ],
        scratch_shapes=[pltpu.VMEM((bb, HEAD_DIM), F32)],
        input_output_aliases=aliases,
        compiler_params=_params(("arbitrary",)),
        name="retention_step",
    )(*in_arrays, *extra)


def _conv_tail(c, gcn_ref, bcn_ref, wpw_ref, bpw_ref, gate):
    mu = jnp.mean(c, axis=-1, keepdims=True)
    d = c - mu
    var = jnp.mean(d * d, axis=-1, keepdims=True)
    y = _silu(d * lax.rsqrt(var + EPS) * gcn_ref[...] + bcn_ref[...])
    y = jnp.dot(y.astype(BF16), wpw_ref[...], preferred_element_type=F32) + bpw_ref[...]
    return (y * gate).astype(BF16)


def _conv_kernel(u_ref, gc_ref, wdw_ref, bdw_ref, gcn_ref, bcn_ref, wpw_ref, bpw_ref,
                 o_ref, st_ref, full_ref, c_ref, sh_ref):
    t = pl.program_id(1)
    tt, ch = u_ref.shape
    pad = HIST_ROWS - (CONV_K - 1)

    @pl.when(t == 0)
    def _():
        full_ref[0:HIST_ROWS, :] = jnp.zeros((HIST_ROWS, ch), F32)

    full_ref[HIST_ROWS:HIST_ROWS + tt, :] = u_ref[...]

    for p in range(SUBLANES):
        span = tt + SUBLANES * ((CONV_K - 1 - p) // SUBLANES)
        sh_ref[p, 0:span, :] = full_ref[pad + p:pad + p + span, :]

    for r0 in range(0, tt, CONV_ROW_BLOCK):
        for cb in range(ch // LANES):
            cs = slice(cb * LANES, (cb + 1) * LANES)
            acc = bdw_ref[:, cs]
            for j in range(CONV_K):
                p, a = j % SUBLANES, j // SUBLANES
                row = r0 + a * SUBLANES
                acc = acc + sh_ref[p, row:row + CONV_ROW_BLOCK, cs] * wdw_ref[j:j + 1, cs]
            c_ref[r0:r0 + CONV_ROW_BLOCK, cs] = acc

    o_ref[...] = _conv_tail(c_ref[...], gcn_ref, bcn_ref, wpw_ref, bpw_ref,
                            gc_ref[...].astype(F32))

    @pl.when(t == pl.num_programs(1) - 1)
    def _():
        st_ref[...] = full_ref[tt + pad:tt + HIST_ROWS, :]

    full_ref[0:HIST_ROWS, :] = full_ref[tt:tt + HIST_ROWS, :]


def _conv_weight_specs(ch, li, n_grid):
    vec = _layer_spec((1, ch), li, n_grid)
    return [_layer_spec((CONV_K, ch), li, n_grid), vec, vec, vec,
            _layer_spec((ch, ch), li, n_grid), vec]


def _conv(u, gc, conv_w, st_prev, li, depth, *, batch, seq, tt):
    m, ch = u.shape
    nt = seq // tt
    tile = pl.BlockSpec((tt, ch), lambda b, t: (b * nt + t, 0))
    in_arrays = [u, gc, *conv_w]
    in_specs = [tile, tile] + _conv_weight_specs(ch, li, 2)
    body, extra, extra_specs, aliases = _alias_args(
        _conv_kernel, len(in_arrays), st_prev, 1)
    return pl.pallas_call(
        body,
        grid=(batch, nt),
        in_specs=in_specs + extra_specs,
        out_specs=[
            tile,
            pl.BlockSpec((None, None, CONV_K - 1, ch), lambda b, t: (li, b, 0, 0)),
        ],
        out_shape=[
            jax.ShapeDtypeStruct((m, ch), BF16),
            jax.ShapeDtypeStruct((depth, batch, CONV_K - 1, ch), F32),
        ],
        scratch_shapes=[
            pltpu.VMEM((HIST_ROWS + tt, ch), F32),
            pltpu.VMEM((tt, ch), F32),
            pltpu.VMEM((SUBLANES, tt + HIST_ROWS - SUBLANES, ch), F32),
        ],
        input_output_aliases=aliases,
        compiler_params=_params(("arbitrary", "arbitrary")),
        name="conv",
    )(*in_arrays, *extra)


def _conv_step_kernel(u_ref, gc_ref, st_ref, wdw_ref, bdw_ref, gcn_ref, bcn_ref, wpw_ref,
                      bpw_ref, o_ref, st_out_ref):
    bb = u_ref.shape[0]
    hist = CONV_K - 1
    u = u_ref[...]
    w = wdw_ref[...]
    c = (jnp.sum(st_ref[...] * w[None, :hist, :], axis=1)
         + u * w[hist:hist + 1, :] + bdw_ref[...])
    o_ref[...] = _conv_tail(c, gcn_ref, bcn_ref, wpw_ref, bpw_ref,
                            gc_ref[...].astype(F32))
    st_out_ref[:, 0:hist - 1, :] = st_ref[:, 1:hist, :]
    for b in range(bb):
        st_out_ref[b, hist - 1:hist, :] = u_ref[b:b + 1, :]


def _conv_step(u, gc, state, conv_w, st_prev, li, *, bb):
    n_b, ch = u.shape
    st_spec = pl.BlockSpec((None, bb, CONV_K - 1, ch), lambda i: (li, i, 0, 0))
    tile = pl.BlockSpec((bb, ch), lambda i: (i, 0))
    in_arrays = [u, gc, state, *conv_w]
    in_specs = [tile, tile, st_spec] + _conv_weight_specs(ch, li, 1)
    body, extra, extra_specs, aliases = _alias_args(
        _conv_step_kernel, len(in_arrays), st_prev, 1)
    return pl.pallas_call(
        body,
        grid=(n_b // bb,),
        in_specs=in_specs + extra_specs,
        out_specs=[tile, st_spec],
        out_shape=[
            jax.ShapeDtypeStruct((n_b, ch), BF16),
            jax.ShapeDtypeStruct(state.shape, F32),
        ],
        input_output_aliases=aliases,
        compiler_params=_params(("arbitrary",)),
        name="conv_step",
    )(*in_arrays, *extra)


def _out_kernel(ret_ref, conv_ref, x_ref, p_ref, wo_ref, wg_ref, wp_ref, gpost_ref,
                gple_ref, y_ref):
    rw = ret_ref.shape[1]
    m = (jnp.dot(ret_ref[...], wo_ref[0:rw, :], preferred_element_type=F32)
         + jnp.dot(conv_ref[...], wo_ref[rw:, :], preferred_element_type=F32))
    h = x_ref[...] + _rms(m, gpost_ref[...])
    gate = jax.nn.sigmoid(
        jnp.dot(h.astype(BF16), wg_ref[...], preferred_element_type=F32))
    e = jnp.dot(p_ref[...].astype(BF16), wp_ref[...], preferred_element_type=F32)
    y_ref[...] = h + gate * _rms(e, gple_ref[...])


def _out_proj(ret, conv, x, p, w_out, w_gate, w_ple, g_post, g_ple, li, *, bm):
    m, d = x.shape
    pd = p.shape[2]
    rows = lambda width: pl.BlockSpec((bm, width), lambda i: (i, 0))
    return pl.pallas_call(
        _out_kernel,
        grid=(m // bm,),
        in_specs=[
            rows(ret.shape[1]), rows(conv.shape[1]), rows(d),
            pl.BlockSpec((None, bm, pd), lambda i: (li, i, 0)),
            _layer_spec(w_out.shape[1:], li, 1, 1), _layer_spec(w_gate.shape[1:], li, 1, 1),
            _layer_spec(w_ple.shape[1:], li, 1, 1),
            _layer_spec((1, d), li, 1), _layer_spec((1, d), li, 1),
        ],
        out_specs=rows(d),
        out_shape=jax.ShapeDtypeStruct((m, d), F32),
        compiler_params=_params(("arbitrary",)),
        name="out_proj",
    )(ret, conv, x, p, w_out, w_gate, w_ple, g_post, g_ple)


def _rope_tables(pos):
    half = HEAD_DIM // 2
    inv = ROPE_BASE ** (-jnp.arange(half, dtype=F32) / half)
    ang = pos.astype(F32)[:, None] * inv[None, :]
    cos, sin = jnp.cos(ang), jnp.sin(ang)
    return jnp.concatenate([cos, cos], axis=-1), jnp.concatenate([-sin, sin], axis=-1)


def kernel(x_prompt, x_sample, state_ret, state_conv, p_prompt, p_sample, w_in, w_out,
           g_ret, w_dw, b_dw, g_cn, b_cn, w_pw2, b_pw2, g_pre, g_post, w_ple, g_ple,
           w_ple_gate):
    depth = w_in.shape[0]
    bp, lp, d = x_prompt.shape
    bs, ls, _ = x_sample.shape
    assert ls == 1 and lp % RET_CHUNK == 0

    w_in_b, w_out_b, w_pw_b = w_in.astype(BF16), w_out.astype(BF16), w_pw2.astype(BF16)
    w_ple_b, w_gate_b = w_ple.astype(BF16), w_ple_gate.astype(BF16)

    log_g = jnp.log1p(-jnp.exp2(-5.0 - jnp.arange(RET_HEADS, dtype=F32)))
    lg_tab = jnp.broadcast_to(log_g[:, None, None], (RET_HEADS, 1, RET_CHUNK))
    gam_tab = jnp.broadcast_to(jnp.exp(log_g)[:, None, None], (RET_HEADS, 1, HEAD_DIM))
    cos_p, sin_p = _rope_tables(jnp.arange(lp, dtype=jnp.int32))
    cos_s, sin_s = _rope_tables(jnp.full((bs,), PAST_LEN, dtype=jnp.int32))

    vec = lambda a: a.reshape(depth, 1, -1)
    g_pre3, g_post3, g_ple3 = vec(g_pre), vec(g_post), vec(g_ple)
    g_ret4 = g_ret.reshape(depth, RET_HEADS, 1, HEAD_DIM)
    conv_w = (w_dw, vec(b_dw), vec(g_cn), vec(b_cn), w_pw_b, vec(b_pw2))
    out_w = (w_out_b, w_gate_b, w_ple_b, g_post3, g_ple3)
    pp = p_prompt.reshape(depth, bp * lp, -1)
    ps = p_sample.reshape(depth, bs, -1)

    bm_in, bm_out, tq, tt = 256, 512, 512, 256
    xp = x_prompt.reshape(bp * lp, d)
    xs = x_sample.reshape(bs, d)
    ret_p = conv_p = ret_s = conv_s = None
    for li in range(depth):
        zr, u, gc = _in_proj(xp, g_pre3, w_in_b, cos_p, sin_p, li,
                             bm=bm_in, pos_blocks=lp // bm_in)
        r_out, ret_p = _retention(zr, g_ret4, lg_tab, ret_p, li, depth,
                                  batch=bp, seq=lp, tq=tq)
        c_out, conv_p = _conv(u, gc, conv_w, conv_p, li, depth, batch=bp, seq=lp, tt=tt)
        xp = _out_proj(r_out, c_out, xp, pp, *out_w, li, bm=bm_out)

        zr, u, gc = _in_proj(xs, g_pre3, w_in_b, cos_s, sin_s, li, bm=bs, pos_blocks=1)
        r_out, ret_s = _retention_step(zr, state_ret, g_ret4, gam_tab, ret_s, li, bb=16)
        c_out, conv_s = _conv_step(u, gc, state_conv, conv_w, conv_s, li, bb=16)
        xs = _out_proj(r_out, c_out, xs, ps, *out_w, li, bm=bs)

    return (xp.reshape(bp, lp, d), xs.reshape(bs, ls, d), ret_p, conv_p, ret_s, conv_s)
```

```python
import functools

import jax
import jax.numpy as jnp
from jax import lax
from jax.experimental import pallas as pl
from jax.experimental.pallas import tpu as pltpu

F32 = jnp.float32
BF16 = jnp.bfloat16

RET_HEADS = 8
HEAD_DIM = 128
RET_WIDTH = RET_HEADS * HEAD_DIM
CONV_K = 31
ROPE_BASE = 10000.0
EPS = 1e-6
PAST_LEN = 16384

VMEM_LIMIT_BYTES = 58 * 1024 * 1024
LANES = 128
SUBLANES = 8
COL_TILE = 1024
RET_CHUNK = 256
CONV_ROW_BLOCK = 64
HIST_ROWS = 32


def _params(sem):
    return pltpu.CompilerParams(dimension_semantics=sem,
                                vmem_limit_bytes=VMEM_LIMIT_BYTES)


def _silu(x):
    return x * jax.nn.sigmoid(x)


def _rms(x, g):
    return x * lax.rsqrt(jnp.mean(x * x, axis=-1, keepdims=True) + EPS) * g


def _layer_spec(shape, li, n_grid, buffered=None):
    zeros = (0,) * len(shape)
    if n_grid == 1:
        index_map = lambda i: (li,) + zeros
    else:
        index_map = lambda i, j: (li,) + zeros
    if buffered is None:
        return pl.BlockSpec((None,) + tuple(shape), index_map)
    return pl.BlockSpec((None,) + tuple(shape), index_map,
                        pipeline_mode=pl.Buffered(buffered))


def _whole_spec(shape, n_grid, buffered=None):
    zeros = (0,) * len(shape)
    index_map = (lambda i: zeros) if n_grid == 1 else (lambda i, j: zeros)
    if buffered is None:
        return pl.BlockSpec(tuple(shape), index_map)
    return pl.BlockSpec(tuple(shape), index_map, pipeline_mode=pl.Buffered(buffered))


def _cast_specs(w, li, n_steps, step_of):
    _, rows, cols = w.shape
    r = rows // n_steps
    assert r * n_steps == rows and r % 16 == 0
    return (pl.BlockSpec((None, r, cols), lambda *g: (li, step_of(*g), 0)),
            pl.BlockSpec((r, cols), lambda *g: (step_of(*g), 0)),
            jax.ShapeDtypeStruct((rows, cols), BF16))


def _with_alias_slot(body, n_in):
    def wrapped(*refs):
        return body(*refs[:n_in], *refs[n_in + 1:])
    return wrapped


def _alias_args(body, n_in, prev, out_idx):
    if prev is None:
        return body, [], [], {}
    return (_with_alias_slot(body, n_in), [prev],
            [pl.BlockSpec(memory_space=pl.ANY)], {n_in: out_idx})


def _in_proj_kernel(x_ref, gpre_ref, w_ref, cos_ref, sin_ref, zr_ref, u_ref, gc_ref):
    hn = _rms(x_ref[...], gpre_ref[...]).astype(BF16)

    def proj(g):
        return jnp.dot(hn, w_ref[:, g * COL_TILE:(g + 1) * COL_TILE],
                       preferred_element_type=F32)

    def rope(z, scale, col0):
        c = cos_ref[...]
        s = sin_ref[...]
        for h in range(RET_HEADS):
            xh = z[:, h * HEAD_DIM:(h + 1) * HEAD_DIM]
            rot = pltpu.roll(xh, HEAD_DIM // 2, axis=1)
            r = xh * c + rot * s
            if scale is not None:
                r = r * scale
            zr_ref[:, col0 + h * HEAD_DIM:col0 + (h + 1) * HEAD_DIM] = r.astype(BF16)

    rope(proj(0), None, 0)
    rope(proj(1), HEAD_DIM ** -0.5, COL_TILE)
    zr_ref[:, 2 * COL_TILE:3 * COL_TILE] = proj(2).astype(BF16)
    zr_ref[:, 3 * COL_TILE:4 * COL_TILE] = _silu(proj(3)).astype(BF16)
    u_ref[...] = proj(4) * jax.nn.sigmoid(proj(5))
    gc_ref[...] = _silu(proj(6)).astype(BF16)


def _in_proj(x, g_pre, w_in, cos_t, sin_t, li, *, bm, pos_blocks):
    m, d = x.shape
    n_cols = w_in.shape[1]
    assert n_cols == 7 * COL_TILE and m % bm == 0
    rows = lambda width: pl.BlockSpec((bm, width), lambda i: (i, 0))
    pos = pl.BlockSpec((bm, HEAD_DIM), lambda i: (i % pos_blocks, 0))
    return pl.pallas_call(
        _in_proj_kernel,
        grid=(m // bm,),
        in_specs=[rows(d), _layer_spec((1, d), li, 1), _whole_spec((d, n_cols), 1, 1),
                  pos, pos],
        out_specs=[rows(4 * COL_TILE), rows(COL_TILE), rows(COL_TILE)],
        out_shape=[
            jax.ShapeDtypeStruct((m, 4 * COL_TILE), BF16),
            jax.ShapeDtypeStruct((m, COL_TILE), F32),
            jax.ShapeDtypeStruct((m, COL_TILE), BF16),
        ],
        compiler_params=_params(("arbitrary",)),
        name="in_proj",
    )(x, g_pre, w_in, cos_t, sin_t)


def _group_norm_gate(o, g_ret, gate):
    mu = jnp.mean(o, axis=-1, keepdims=True)
    d = o - mu
    var = jnp.mean(d * d, axis=-1, keepdims=True)
    return (d * lax.rsqrt(var + EPS)) * g_ret * gate


def _retention_kernel(q_ref, k_ref, v_ref, gr_ref, gret_ref, lg_ref, *rest, n_cast):
    cast_in, rest = rest[:n_cast], rest[n_cast:]
    o_ref, s_out_ref = rest[:2]
    cast_out = rest[2:2 + n_cast]
    s_ref, decay_ref, cross_ref, kdec_ref = rest[2 + n_cast:]
    for src, dst in zip(cast_in, cast_out):
        dst[...] = src[...].astype(BF16)

    c = RET_CHUNK
    b = pl.program_id(0)
    t = pl.program_id(1)

    @pl.when((b == 0) & (t == 0))
    def _():
        row = lax.broadcasted_iota(jnp.int32, (c, c), 0)
        col = lax.broadcasted_iota(jnp.int32, (c, c), 1)
        rel = (row - col).astype(F32)
        idx = lax.broadcasted_iota(jnp.int32, (c, LANES), 0).astype(F32)
        for h in range(RET_HEADS):
            lg = lg_ref[h]
            decay_ref[h] = jnp.where(rel >= 0, jnp.exp(lg * jnp.maximum(rel, 0.0)), 0.0)
            cross_ref[h] = jnp.exp(lg[:, :LANES] * (idx + 1.0))
            kdec_ref[h] = jnp.exp(lg[:, :LANES] * (c - 1.0 - idx))

    @pl.when(t == 0)
    def _():
        s_ref[...] = jnp.zeros(s_ref.shape, F32)

    for ci in range(q_ref.shape[0] // c):
        rows = slice(ci * c, (ci + 1) * c)
        for h in range(RET_HEADS):
            cols = slice(h * HEAD_DIM, (h + 1) * HEAD_DIM)
            q = q_ref[rows, cols]
            k = k_ref[rows, cols]
            v = v_ref[rows, cols]
            s = s_ref[h]
            g_chunk = jnp.exp(lg_ref[h][:, :LANES] * float(c))
            scores = lax.dot_general(q, k, (((1,), (1,)), ((), ())),
                                     preferred_element_type=F32) * decay_ref[h]
            o = jnp.dot(scores.astype(BF16), v, preferred_element_type=F32)
            o = o + jnp.dot(q, s.astype(BF16), preferred_element_type=F32) * cross_ref[h]
            kd = (k.astype(F32) * kdec_ref[h]).astype(BF16)
            s_ref[h] = g_chunk * s + lax.dot_general(
                kd, v, (((0,), (0,)), ((), ())), preferred_element_type=F32)
            o_ref[rows, cols] = _group_norm_gate(
                o, gret_ref[h], gr_ref[rows, cols].astype(F32)).astype(BF16)

    @pl.when(t == pl.num_programs(1) - 1)
    def _():
        s_out_ref[...] = s_ref[...]


def _retention(zr, g_ret, lg_tab, cast_w, s_prev, li, depth, *, batch, seq, tq):
    m = zr.shape[0]
    nt = seq // tq
    blk = lambda g: pl.BlockSpec((tq, RET_WIDTH), lambda b, t: (b * nt + t, g))
    casts = [_cast_specs(w, li, batch * nt, lambda b, t: b * nt + t) for w in cast_w]
    in_arrays = [zr, zr, zr, zr, g_ret, lg_tab, *cast_w]
    in_specs = [blk(0), blk(1), blk(2), blk(3),
                _layer_spec((RET_HEADS, 1, HEAD_DIM), li, 2),
                pl.BlockSpec(lg_tab.shape, lambda b, t: (0, 0, 0))] + [c[0] for c in casts]
    body, extra, extra_specs, aliases = _alias_args(
        functools.partial(_retention_kernel, n_cast=len(cast_w)),
        len(in_arrays), s_prev, 1)
    return pl.pallas_call(
        body,
        grid=(batch, nt),
        in_specs=in_specs + extra_specs,
        out_specs=[
            pl.BlockSpec((tq, RET_WIDTH), lambda b, t: (b * nt + t, 0)),
            pl.BlockSpec((None, None, RET_HEADS, HEAD_DIM, HEAD_DIM),
                         lambda b, t: (li, b, 0, 0, 0)),
        ] + [c[1] for c in casts],
        out_shape=[
            jax.ShapeDtypeStruct((m, RET_WIDTH), BF16),
            jax.ShapeDtypeStruct((depth, batch, RET_HEADS, HEAD_DIM, HEAD_DIM), F32),
        ] + [c[2] for c in casts],
        scratch_shapes=[
            pltpu.VMEM((RET_HEADS, HEAD_DIM, HEAD_DIM), F32),
            pltpu.VMEM((RET_HEADS, RET_CHUNK, RET_CHUNK), F32),
            pltpu.VMEM((RET_HEADS, RET_CHUNK, LANES), F32),
            pltpu.VMEM((RET_HEADS, RET_CHUNK, LANES), F32),
        ],
        input_output_aliases=aliases,
        compiler_params=_params(("arbitrary", "arbitrary")),
        name="retention",
    )(*in_arrays, *extra)


def _retention_step_kernel(zr_ref, s_ref, gret_ref, gam_ref, o_ref, s_out_ref, oc_ref,
                           *, bb):
    i = pl.program_id(0)
    n_b = zr_ref.shape[0]
    rows = pl.ds(pl.multiple_of(i * bb, bb), bb)
    shift = (n_b - i * bb) % n_b
    for h in range(RET_HEADS):
        hs = lambda g: slice((g * RET_HEADS + h) * HEAD_DIM,
                             (g * RET_HEADS + h + 1) * HEAD_DIM)
        q_t = pltpu.roll(zr_ref[:, hs(0)].astype(F32).T, shift, axis=1)
        k_t = pltpu.roll(zr_ref[:, hs(1)].astype(F32).T, shift, axis=1)
        q = zr_ref[rows, hs(0)].astype(F32)
        k = zr_ref[rows, hs(1)].astype(F32)
        v = zr_ref[rows, hs(2)].astype(F32)
        gate = zr_ref[rows, hs(3)].astype(F32)
        gam = gam_ref[h]
        for b in range(bb):
            s = s_ref[b, h]
            oc_ref[b:b + 1, :] = jnp.sum(q_t[:, b:b + 1] * s, axis=0, keepdims=True)
            s_out_ref[b, h] = gam * s + k_t[:, b:b + 1] * v[b:b + 1, :]
        o = jnp.sum(q * k, axis=-1, keepdims=True) * v + oc_ref[...] * gam
        o_ref[:, hs(0)] = _group_norm_gate(o, gret_ref[h], gate).astype(BF16)


def _retention_step(zr, state, g_ret, gam_tab, s_prev, li, *, bb):
    n_b = zr.shape[0]
    st_spec = pl.BlockSpec((None, bb, RET_HEADS, HEAD_DIM, HEAD_DIM),
                           lambda i: (li, i, 0, 0, 0))
    in_arrays = [zr, state, g_ret, gam_tab]
    in_specs = [pl.BlockSpec(zr.shape, lambda i: (0, 0)), st_spec,
                _layer_spec((RET_HEADS, 1, HEAD_DIM), li, 1),
                pl.BlockSpec(gam_tab.shape, lambda i: (0, 0, 0))]
    body, extra, extra_specs, aliases = _alias_args(
        functools.partial(_retention_step_kernel, bb=bb), len(in_arrays), s_prev, 1)
    return pl.pallas_call(
        body,
        grid=(n_b // bb,),
        in_specs=in_specs + extra_specs,
        out_specs=[pl.BlockSpec((bb, RET_WIDTH), lambda i: (i, 0)), st_spec],
        out_shape=[
            jax.ShapeDtypeStruct((n_b, RET_WIDTH), BF16),
            jax.ShapeDtypeStruct(state.shape, F32),
        ],
        scratch_shapes=[pltpu.VMEM((bb, HEAD_DIM), F32)],
        input_output_aliases=aliases,
        compiler_params=_params(("arbitrary",)),
        name="retention_step",
    )(*in_arrays, *extra)


def _conv_tail(c, gcn_ref, bcn_ref, wpw_ref, bpw_ref, gate):
    mu = jnp.mean(c, axis=-1, keepdims=True)
    d = c - mu
    var = jnp.mean(d * d, axis=-1, keepdims=True)
    y = _silu(d * lax.rsqrt(var + EPS) * gcn_ref[...] + bcn_ref[...])
    y = jnp.dot(y.astype(BF16), wpw_ref[...], preferred_element_type=F32) + bpw_ref[...]
    return (y * gate).astype(BF16)


def _conv_kernel(u_ref, gc_ref, wdw_ref, bdw_ref, gcn_ref, bcn_ref, wpw_ref, bpw_ref,
                 *rest, n_cast):
    cast_in, rest = rest[:n_cast], rest[n_cast:]
    o_ref, st_ref = rest[:2]
    cast_out = rest[2:2 + n_cast]
    full_ref, c_ref, sh_ref = rest[2 + n_cast:]
    for src, dst in zip(cast_in, cast_out):
        dst[...] = src[...].astype(BF16)

    t = pl.program_id(1)
    tt, ch = u_ref.shape
    pad = HIST_ROWS - (CONV_K - 1)

    @pl.when(t == 0)
    def _():
        full_ref[0:HIST_ROWS, :] = jnp.zeros((HIST_ROWS, ch), F32)

    full_ref[HIST_ROWS:HIST_ROWS + tt, :] = u_ref[...]

    for p in range(SUBLANES):
        span = tt + SUBLANES * ((CONV_K - 1 - p) // SUBLANES)
        sh_ref[p, 0:span, :] = full_ref[pad + p:pad + p + span, :]

    for r0 in range(0, tt, CONV_ROW_BLOCK):
        for cb in range(ch // LANES):
            cs = slice(cb * LANES, (cb + 1) * LANES)
            acc = bdw_ref[:, cs]
            for j in range(CONV_K):
                p, a = j % SUBLANES, j // SUBLANES
                row = r0 + a * SUBLANES
                acc = acc + sh_ref[p, row:row + CONV_ROW_BLOCK, cs] * wdw_ref[j:j + 1, cs]
            c_ref[r0:r0 + CONV_ROW_BLOCK, cs] = acc

    o_ref[...] = _conv_tail(c_ref[...], gcn_ref, bcn_ref, wpw_ref, bpw_ref,
                            gc_ref[...].astype(F32))

    @pl.when(t == pl.num_programs(1) - 1)
    def _():
        st_ref[...] = full_ref[tt + pad:tt + HIST_ROWS, :]

    full_ref[0:HIST_ROWS, :] = full_ref[tt:tt + HIST_ROWS, :]


def _conv_weight_specs(ch, li, n_grid):
    vec = _layer_spec((1, ch), li, n_grid)
    return [_layer_spec((CONV_K, ch), li, n_grid), vec, vec, vec,
            _whole_spec((ch, ch), n_grid), vec]


def _conv(u, gc, conv_w, cast_w, cast_li, st_prev, li, depth, *, batch, seq, tt):
    m, ch = u.shape
    nt = seq // tt
    tile = pl.BlockSpec((tt, ch), lambda b, t: (b * nt + t, 0))
    casts = [_cast_specs(w, cast_li, batch * nt, lambda b, t: b * nt + t) for w in cast_w]
    in_arrays = [u, gc, *conv_w, *cast_w]
    in_specs = [tile, tile] + _conv_weight_specs(ch, li, 2) + [c[0] for c in casts]
    body, extra, extra_specs, aliases = _alias_args(
        functools.partial(_conv_kernel, n_cast=len(cast_w)), len(in_arrays), st_prev, 1)
    return pl.pallas_call(
        body,
        grid=(batch, nt),
        in_specs=in_specs + extra_specs,
        out_specs=[
            tile,
            pl.BlockSpec((None, None, CONV_K - 1, ch), lambda b, t: (li, b, 0, 0)),
        ] + [c[1] for c in casts],
        out_shape=[
            jax.ShapeDtypeStruct((m, ch), BF16),
            jax.ShapeDtypeStruct((depth, batch, CONV_K - 1, ch), F32),
        ] + [c[2] for c in casts],
        scratch_shapes=[
            pltpu.VMEM((HIST_ROWS + tt, ch), F32),
            pltpu.VMEM((tt, ch), F32),
            pltpu.VMEM((SUBLANES, tt + HIST_ROWS - SUBLANES, ch), F32),
        ],
        input_output_aliases=aliases,
        compiler_params=_params(("arbitrary", "arbitrary")),
        name="conv",
    )(*in_arrays, *extra)


def _conv_step_kernel(u_ref, gc_ref, st_ref, wdw_ref, bdw_ref, gcn_ref, bcn_ref, wpw_ref,
                      bpw_ref, o_ref, st_out_ref):
    hist = CONV_K - 1
    u = u_ref[...]
    c = u * wdw_ref[hist:hist + 1, :] + bdw_ref[...]
    for j in range(hist):
        c = c + st_ref[j] * wdw_ref[j:j + 1, :]
    o_ref[...] = _conv_tail(c, gcn_ref, bcn_ref, wpw_ref, bpw_ref,
                            gc_ref[...].astype(F32))
    for j in range(hist - 1):
        st_out_ref[j] = st_ref[j + 1]
    st_out_ref[hist - 1] = u


def _conv_step(u, gc, state, conv_w, st_prev, li, *, bb):
    n_b, ch = u.shape
    st_spec = pl.BlockSpec((None, CONV_K - 1, bb, ch), lambda i: (li, 0, i, 0))
    tile = pl.BlockSpec((bb, ch), lambda i: (i, 0))
    in_arrays = [u, gc, state, *conv_w]
    in_specs = [tile, tile, st_spec] + _conv_weight_specs(ch, li, 1)
    body, extra, extra_specs, aliases = _alias_args(
        _conv_step_kernel, len(in_arrays), st_prev, 1)
    return pl.pallas_call(
        body,
        grid=(n_b // bb,),
        in_specs=in_specs + extra_specs,
        out_specs=[tile, st_spec],
        out_shape=[
            jax.ShapeDtypeStruct((n_b, ch), BF16),
            jax.ShapeDtypeStruct(state.shape, F32),
        ],
        input_output_aliases=aliases,
        compiler_params=_params(("arbitrary",)),
        name="conv_step",
    )(*in_arrays, *extra)


def _out_kernel(ret_ref, conv_ref, x_ref, p_ref, wo_ref, wg_ref, wp_ref, gpost_ref,
                gple_ref, y_ref):
    rw = ret_ref.shape[1]
    m = (jnp.dot(ret_ref[...], wo_ref[0:rw, :], preferred_element_type=F32)
         + jnp.dot(conv_ref[...], wo_ref[rw:, :], preferred_element_type=F32))
    h = x_ref[...] + _rms(m, gpost_ref[...])
    gate = jax.nn.sigmoid(
        jnp.dot(h.astype(BF16), wg_ref[...], preferred_element_type=F32))
    e = jnp.dot(p_ref[...].astype(BF16), wp_ref[...], preferred_element_type=F32)
    y_ref[...] = h + gate * _rms(e, gple_ref[...])


def _out_proj(ret, conv, x, p, w_out, w_gate, w_ple, g_post, g_ple, li, *, bm):
    m, d = x.shape
    pd = p.shape[2]
    rows = lambda width: pl.BlockSpec((bm, width), lambda i: (i, 0))
    return pl.pallas_call(
        _out_kernel,
        grid=(m // bm,),
        in_specs=[
            rows(ret.shape[1]), rows(conv.shape[1]), rows(d),
            pl.BlockSpec((None, bm, pd), lambda i: (li, i, 0)),
            _whole_spec(w_out.shape, 1, 1), _whole_spec(w_gate.shape, 1, 1),
            _whole_spec(w_ple.shape, 1, 1),
            _layer_spec((1, d), li, 1), _layer_spec((1, d), li, 1),
        ],
        out_specs=rows(d),
        out_shape=jax.ShapeDtypeStruct((m, d), F32),
        compiler_params=_params(("arbitrary",)),
        name="out_proj",
    )(ret, conv, x, p, w_out, w_gate, w_ple, g_post, g_ple)


def _rope_tables(pos):
    half = HEAD_DIM // 2
    inv = ROPE_BASE ** (-jnp.arange(half, dtype=F32) / half)
    ang = pos.astype(F32)[:, None] * inv[None, :]
    cos, sin = jnp.cos(ang), jnp.sin(ang)
    return jnp.concatenate([cos, cos], axis=-1), jnp.concatenate([-sin, sin], axis=-1)


def kernel(x_prompt, x_sample, state_ret, state_conv, p_prompt, p_sample, w_in, w_out,
           g_ret, w_dw, b_dw, g_cn, b_cn, w_pw2, b_pw2, g_pre, g_post, w_ple, g_ple,
           w_ple_gate):
    depth = w_in.shape[0]
    bp, lp, d = x_prompt.shape
    bs, ls, _ = x_sample.shape
    assert ls == 1 and lp % RET_CHUNK == 0

    w_in_b = w_in[0].astype(BF16)

    log_g = jnp.log1p(-jnp.exp2(-5.0 - jnp.arange(RET_HEADS, dtype=F32)))
    lg_tab = jnp.broadcast_to(log_g[:, None, None], (RET_HEADS, 1, RET_CHUNK))
    gam_tab = jnp.broadcast_to(jnp.exp(log_g)[:, None, None], (RET_HEADS, 1, HEAD_DIM))
    cos_p, sin_p = _rope_tables(jnp.arange(lp, dtype=jnp.int32))
    cos_s, sin_s = _rope_tables(jnp.full((bs,), PAST_LEN, dtype=jnp.int32))

    vec = lambda a: a.reshape(depth, 1, -1)
    g_pre3, g_post3, g_ple3 = vec(g_pre), vec(g_post), vec(g_ple)
    g_ret4 = g_ret.reshape(depth, RET_HEADS, 1, HEAD_DIM)
    conv_vecs = (vec(b_dw), vec(g_cn), vec(b_cn))
    b_pw3 = vec(b_pw2)
    state_conv_t = jnp.transpose(state_conv, (0, 2, 1, 3))
    pp = p_prompt.reshape(depth, bp * lp, -1)
    ps = p_sample.reshape(depth, bs, -1)

    bm_in, bm_out, tq, tt = 256, 512, 512, 256
    xp = x_prompt.reshape(bp * lp, d)
    xs = x_sample.reshape(bs, d)
    ret_p = conv_p = ret_s = conv_s = None
    for li in range(depth):
        zr, u, gc = _in_proj(xp, g_pre3, w_in_b, cos_p, sin_p, li,
                             bm=bm_in, pos_blocks=lp // bm_in)
        r_out, ret_p, w_out_b, w_gate_b, w_ple_b, w_pw_b = _retention(
            zr, g_ret4, lg_tab, (w_out, w_ple_gate, w_ple, w_pw2), ret_p, li, depth,
            batch=bp, seq=lp, tq=tq)
        conv_w = (w_dw, *conv_vecs, w_pw_b, b_pw3)
        out_w = (w_out_b, w_gate_b, w_ple_b, g_post3, g_ple3)
        w_in_cur = w_in_b
        next_w = (w_in,) if li + 1 < depth else ()
        c_out, conv_p, *w_in_next = _conv(u, gc, conv_w, next_w, li + 1, conv_p, li, depth,
                                          batch=bp, seq=lp, tt=tt)
        if w_in_next:
            w_in_b = w_in_next[0]
        xp = _out_proj(r_out, c_out, xp, pp, *out_w, li, bm=bm_out)

        zr, u, gc = _in_proj(xs, g_pre3, w_in_cur, cos_s, sin_s, li, bm=bs, pos_blocks=1)
        r_out, ret_s = _retention_step(zr, state_ret, g_ret4, gam_tab, ret_s, li, bb=16)
        c_out, conv_s = _conv_step(u, gc, state_conv_t, conv_w, conv_s, li, bb=16)
        xs = _out_proj(r_out, c_out, xs, ps, *out_w, li, bm=bs)

    return (xp.reshape(bp, lp, d), xs.reshape(bs, ls, d), ret_p, conv_p, ret_s,
            jnp.transpose(conv_s, (0, 2, 1, 3)))
```

```python
import functools

import jax
import jax.numpy as jnp
from jax import lax
from jax.experimental import pallas as pl
from jax.experimental.pallas import tpu as pltpu

F32 = jnp.float32
BF16 = jnp.bfloat16

RET_HEADS = 8
HEAD_DIM = 128
RET_WIDTH = RET_HEADS * HEAD_DIM
CONV_K = 31
ROPE_BASE = 10000.0
EPS = 1e-6
PAST_LEN = 16384

VMEM_LIMIT_BYTES = 58 * 1024 * 1024
LANES = 128
SUBLANES = 8
COL_TILE = 1024
RET_CHUNK = 256
CONV_ROW_BLOCK = 64
HIST_ROWS = 32


def _params(sem):
    return pltpu.CompilerParams(dimension_semantics=sem,
                                vmem_limit_bytes=VMEM_LIMIT_BYTES)


def _silu(x):
    return x * jax.nn.sigmoid(x)


def _rms(x, g):
    return x * lax.rsqrt(jnp.mean(x * x, axis=-1, keepdims=True) + EPS) * g


def _layer_spec(shape, li, n_grid, buffered=None):
    zeros = (0,) * len(shape)
    if n_grid == 1:
        index_map = lambda i: (li,) + zeros
    else:
        index_map = lambda i, j: (li,) + zeros
    if buffered is None:
        return pl.BlockSpec((None,) + tuple(shape), index_map)
    return pl.BlockSpec((None,) + tuple(shape), index_map,
                        pipeline_mode=pl.Buffered(buffered))


def _whole_spec(shape, n_grid, buffered=None):
    zeros = (0,) * len(shape)
    index_map = (lambda i: zeros) if n_grid == 1 else (lambda i, j: zeros)
    if buffered is None:
        return pl.BlockSpec(tuple(shape), index_map)
    return pl.BlockSpec(tuple(shape), index_map, pipeline_mode=pl.Buffered(buffered))


def _cast_specs(w, li, n_steps, step_of):
    _, rows, cols = w.shape
    r = rows // n_steps
    assert r * n_steps == rows and r % 16 == 0
    return (pl.BlockSpec((None, r, cols), lambda *g: (li, step_of(*g), 0)),
            pl.BlockSpec((r, cols), lambda *g: (step_of(*g), 0)),
            jax.ShapeDtypeStruct((rows, cols), BF16))


def _with_alias_slot(body, n_in):
    def wrapped(*refs):
        return body(*refs[:n_in], *refs[n_in + 1:])
    return wrapped


def _alias_args(body, n_in, prev, out_idx):
    if prev is None:
        return body, [], [], {}
    return (_with_alias_slot(body, n_in), [prev],
            [pl.BlockSpec(memory_space=pl.ANY)], {n_in: out_idx})


def _in_proj_tile(x_ref, gpre_ref, w_ref, cos_ref, sin_ref, zr_ref, u_ref, gc_ref):
    hn = _rms(x_ref[...], gpre_ref[...]).astype(BF16)

    def proj(g):
        return jnp.dot(hn, w_ref[:, g * COL_TILE:(g + 1) * COL_TILE],
                       preferred_element_type=F32)

    def rope(z, scale, col0):
        c = cos_ref[...]
        s = sin_ref[...]
        for h in range(RET_HEADS):
            xh = z[:, h * HEAD_DIM:(h + 1) * HEAD_DIM]
            rot = pltpu.roll(xh, HEAD_DIM // 2, axis=1)
            r = xh * c + rot * s
            if scale is not None:
                r = r * scale
            zr_ref[:, col0 + h * HEAD_DIM:col0 + (h + 1) * HEAD_DIM] = r.astype(BF16)

    rope(proj(0), None, 0)
    rope(proj(1), HEAD_DIM ** -0.5, COL_TILE)
    zr_ref[:, 2 * COL_TILE:3 * COL_TILE] = proj(2).astype(BF16)
    zr_ref[:, 3 * COL_TILE:4 * COL_TILE] = _silu(proj(3)).astype(BF16)
    u_ref[...] = proj(4) * jax.nn.sigmoid(proj(5))
    gc_ref[...] = _silu(proj(6)).astype(BF16)


def _in_proj_kernel(xp_ref, xs_ref, gpre_ref, w_ref, cosp_ref, sinp_ref, coss_ref, sins_ref,
                    zrp_ref, up_ref, gcp_ref, zrs_ref, us_ref, gcs_ref):
    i = pl.program_id(0)
    last = pl.num_programs(0) - 1

    @pl.when(i < last)
    def _():
        _in_proj_tile(xp_ref, gpre_ref, w_ref, cosp_ref, sinp_ref, zrp_ref, up_ref, gcp_ref)

    @pl.when(i == last)
    def _():
        _in_proj_tile(xs_ref, gpre_ref, w_ref, coss_ref, sins_ref, zrs_ref, us_ref, gcs_ref)


def _group_out_shapes(m):
    return [
        jax.ShapeDtypeStruct((m, 4 * COL_TILE), BF16),
        jax.ShapeDtypeStruct((m, COL_TILE), F32),
        jax.ShapeDtypeStruct((m, COL_TILE), BF16),
    ]


def _in_proj(xp, xs, g_pre, w_in, rope_p, rope_s, li, *, bm, pos_blocks):
    m, d = xp.shape
    ms = xs.shape[0]
    n_cols = w_in.shape[1]
    n_tiles = m // bm
    assert n_cols == 7 * COL_TILE and m % bm == 0
    tile = lambda i: jnp.minimum(i, n_tiles - 1)
    rows = lambda width: pl.BlockSpec((bm, width), lambda i: (tile(i), 0))
    pos = pl.BlockSpec((bm, HEAD_DIM), lambda i: (tile(i) % pos_blocks, 0))
    whole = lambda width: _whole_spec((ms, width), 1)
    return pl.pallas_call(
        _in_proj_kernel,
        grid=(n_tiles + 1,),
        in_specs=[rows(d), whole(d), _layer_spec((1, d), li, 1),
                  _whole_spec((d, n_cols), 1, 1), pos, pos,
                  whole(HEAD_DIM), whole(HEAD_DIM)],
        out_specs=[rows(4 * COL_TILE), rows(COL_TILE), rows(COL_TILE),
                   whole(4 * COL_TILE), whole(COL_TILE), whole(COL_TILE)],
        out_shape=_group_out_shapes(m) + _group_out_shapes(ms),
        compiler_params=_params(("arbitrary",)),
        name="in_proj",
    )(xp, xs, g_pre, w_in, *rope_p, *rope_s)


def _group_norm_gate(o, g_ret, gate):
    mu = jnp.mean(o, axis=-1, keepdims=True)
    d = o - mu
    var = jnp.mean(d * d, axis=-1, keepdims=True)
    return (d * lax.rsqrt(var + EPS)) * g_ret * gate


def _retention_kernel(q_ref, k_ref, v_ref, gr_ref, gret_ref, lg_ref, *rest, n_cast):
    cast_in, rest = rest[:n_cast], rest[n_cast:]
    o_ref, s_out_ref = rest[:2]
    cast_out = rest[2:2 + n_cast]
    s_ref, decay_ref, cross_ref, kdec_ref = rest[2 + n_cast:]
    for src, dst in zip(cast_in, cast_out):
        dst[...] = src[...].astype(BF16)

    c = RET_CHUNK
    b = pl.program_id(0)
    t = pl.program_id(1)

    @pl.when((b == 0) & (t == 0))
    def _():
        row = lax.broadcasted_iota(jnp.int32, (c, c), 0)
        col = lax.broadcasted_iota(jnp.int32, (c, c), 1)
        rel = (row - col).astype(F32)
        idx = lax.broadcasted_iota(jnp.int32, (c, LANES), 0).astype(F32)
        for h in range(RET_HEADS):
            lg = lg_ref[h]
            decay_ref[h] = jnp.where(rel >= 0, jnp.exp(lg * jnp.maximum(rel, 0.0)), 0.0)
            cross_ref[h] = jnp.exp(lg[:, :LANES] * (idx + 1.0))
            kdec_ref[h] = jnp.exp(lg[:, :LANES] * (c - 1.0 - idx))

    @pl.when(t == 0)
    def _():
        s_ref[...] = jnp.zeros(s_ref.shape, F32)

    for ci in range(q_ref.shape[0] // c):
        rows = slice(ci * c, (ci + 1) * c)
        for h in range(RET_HEADS):
            cols = slice(h * HEAD_DIM, (h + 1) * HEAD_DIM)
            q = q_ref[rows, cols]
            k = k_ref[rows, cols]
            v = v_ref[rows, cols]
            s = s_ref[h]
            g_chunk = jnp.exp(lg_ref[h][:, :LANES] * float(c))
            scores = lax.dot_general(q, k, (((1,), (1,)), ((), ())),
                                     preferred_element_type=F32) * decay_ref[h]
            o = jnp.dot(scores.astype(BF16), v, preferred_element_type=F32)
            o = o + jnp.dot(q, s.astype(BF16), preferred_element_type=F32) * cross_ref[h]
            kd = (k.astype(F32) * kdec_ref[h]).astype(BF16)
            s_ref[h] = g_chunk * s + lax.dot_general(
                kd, v, (((0,), (0,)), ((), ())), preferred_element_type=F32)
            o_ref[rows, cols] = _group_norm_gate(
                o, gret_ref[h], gr_ref[rows, cols].astype(F32)).astype(BF16)

    @pl.when(t == pl.num_programs(1) - 1)
    def _():
        s_out_ref[...] = s_ref[...]


def _retention(zr, g_ret, lg_tab, cast_w, s_prev, li, depth, *, batch, seq, tq):
    m = zr.shape[0]
    nt = seq // tq
    blk = lambda g: pl.BlockSpec((tq, RET_WIDTH), lambda b, t: (b * nt + t, g))
    casts = [_cast_specs(w, li, batch * nt, lambda b, t: b * nt + t) for w in cast_w]
    in_arrays = [zr, zr, zr, zr, g_ret, lg_tab, *cast_w]
    in_specs = [blk(0), blk(1), blk(2), blk(3),
                _layer_spec((RET_HEADS, 1, HEAD_DIM), li, 2),
                pl.BlockSpec(lg_tab.shape, lambda b, t: (0, 0, 0))] + [c[0] for c in casts]
    body, extra, extra_specs, aliases = _alias_args(
        functools.partial(_retention_kernel, n_cast=len(cast_w)),
        len(in_arrays), s_prev, 1)
    return pl.pallas_call(
        body,
        grid=(batch, nt),
        in_specs=in_specs + extra_specs,
        out_specs=[
            pl.BlockSpec((tq, RET_WIDTH), lambda b, t: (b * nt + t, 0)),
            pl.BlockSpec((None, None, RET_HEADS, HEAD_DIM, HEAD_DIM),
                         lambda b, t: (li, b, 0, 0, 0)),
        ] + [c[1] for c in casts],
        out_shape=[
            jax.ShapeDtypeStruct((m, RET_WIDTH), BF16),
            jax.ShapeDtypeStruct((depth, batch, RET_HEADS, HEAD_DIM, HEAD_DIM), F32),
        ] + [c[2] for c in casts],
        scratch_shapes=[
            pltpu.VMEM((RET_HEADS, HEAD_DIM, HEAD_DIM), F32),
            pltpu.VMEM((RET_HEADS, RET_CHUNK, RET_CHUNK), F32),
            pltpu.VMEM((RET_HEADS, RET_CHUNK, LANES), F32),
            pltpu.VMEM((RET_HEADS, RET_CHUNK, LANES), F32),
        ],
        input_output_aliases=aliases,
        compiler_params=_params(("arbitrary", "arbitrary")),
        name="retention",
    )(*in_arrays, *extra)


def _retention_step_kernel(zr_ref, s_ref, gret_ref, gam_ref, o_ref, s_out_ref, oc_ref,
                           *, bb):
    i = pl.program_id(0)
    n_b = zr_ref.shape[0]
    rows = pl.ds(pl.multiple_of(i * bb, bb), bb)
    shift = (n_b - i * bb) % n_b
    for h in range(RET_HEADS):
        hs = lambda g: slice((g * RET_HEADS + h) * HEAD_DIM,
                             (g * RET_HEADS + h + 1) * HEAD_DIM)
        q_t = pltpu.roll(zr_ref[:, hs(0)].astype(F32).T, shift, axis=1)
        k_t = pltpu.roll(zr_ref[:, hs(1)].astype(F32).T, shift, axis=1)
        q = zr_ref[rows, hs(0)].astype(F32)
        k = zr_ref[rows, hs(1)].astype(F32)
        v = zr_ref[rows, hs(2)].astype(F32)
        gate = zr_ref[rows, hs(3)].astype(F32)
        gam = gam_ref[h]
        for b in range(bb):
            s = s_ref[b, h]
            oc_ref[b:b + 1, :] = jnp.sum(q_t[:, b:b + 1] * s, axis=0, keepdims=True)
            s_out_ref[b, h] = gam * s + k_t[:, b:b + 1] * v[b:b + 1, :]
        o = jnp.sum(q * k, axis=-1, keepdims=True) * v + oc_ref[...] * gam
        o_ref[:, hs(0)] = _group_norm_gate(o, gret_ref[h], gate).astype(BF16)


def _retention_step(zr, state, g_ret, gam_tab, s_prev, li, *, bb):
    n_b = zr.shape[0]
    st_spec = pl.BlockSpec((None, bb, RET_HEADS, HEAD_DIM, HEAD_DIM),
                           lambda i: (li, i, 0, 0, 0))
    in_arrays = [zr, state, g_ret, gam_tab]
    in_specs = [pl.BlockSpec(zr.shape, lambda i: (0, 0)), st_spec,
                _layer_spec((RET_HEADS, 1, HEAD_DIM), li, 1),
                pl.BlockSpec(gam_tab.shape, lambda i: (0, 0, 0))]
    body, extra, extra_specs, aliases = _alias_args(
        functools.partial(_retention_step_kernel, bb=bb), len(in_arrays), s_prev, 1)
    return pl.pallas_call(
        body,
        grid=(n_b // bb,),
        in_specs=in_specs + extra_specs,
        out_specs=[pl.BlockSpec((bb, RET_WIDTH), lambda i: (i, 0)), st_spec],
        out_shape=[
            jax.ShapeDtypeStruct((n_b, RET_WIDTH), BF16),
            jax.ShapeDtypeStruct(state.shape, F32),
        ],
        scratch_shapes=[pltpu.VMEM((bb, HEAD_DIM), F32)],
        input_output_aliases=aliases,
        compiler_params=_params(("arbitrary",)),
        name="retention_step",
    )(*in_arrays, *extra)


def _conv_tail(c, gcn_ref, bcn_ref, wpw_ref, bpw_ref, gate):
    mu = jnp.mean(c, axis=-1, keepdims=True)
    d = c - mu
    var = jnp.mean(d * d, axis=-1, keepdims=True)
    y = _silu(d * lax.rsqrt(var + EPS) * gcn_ref[...] + bcn_ref[...])
    y = jnp.dot(y.astype(BF16), wpw_ref[...], preferred_element_type=F32) + bpw_ref[...]
    return (y * gate).astype(BF16)


def _conv_kernel(u_ref, gc_ref, wdw_ref, bdw_ref, gcn_ref, bcn_ref, wpw_ref, bpw_ref,
                 *rest, n_cast):
    cast_in, rest = rest[:n_cast], rest[n_cast:]
    o_ref, st_ref = rest[:2]
    cast_out = rest[2:2 + n_cast]
    full_ref, c_ref, sh_ref = rest[2 + n_cast:]
    for src, dst in zip(cast_in, cast_out):
        dst[...] = src[...].astype(BF16)

    t = pl.program_id(1)
    tt, ch = u_ref.shape
    pad = HIST_ROWS - (CONV_K - 1)

    @pl.when(t == 0)
    def _():
        full_ref[0:HIST_ROWS, :] = jnp.zeros((HIST_ROWS, ch), F32)

    full_ref[HIST_ROWS:HIST_ROWS + tt, :] = u_ref[...]

    for p in range(SUBLANES):
        span = tt + SUBLANES * ((CONV_K - 1 - p) // SUBLANES)
        sh_ref[p, 0:span, :] = full_ref[pad + p:pad + p + span, :]

    for r0 in range(0, tt, CONV_ROW_BLOCK):
        for cb in range(ch // LANES):
            cs = slice(cb * LANES, (cb + 1) * LANES)
            acc = bdw_ref[:, cs]
            for j in range(CONV_K):
                p, a = j % SUBLANES, j // SUBLANES
                row = r0 + a * SUBLANES
                acc = acc + sh_ref[p, row:row + CONV_ROW_BLOCK, cs] * wdw_ref[j:j + 1, cs]
            c_ref[r0:r0 + CONV_ROW_BLOCK, cs] = acc

    o_ref[...] = _conv_tail(c_ref[...], gcn_ref, bcn_ref, wpw_ref, bpw_ref,
                            gc_ref[...].astype(F32))

    @pl.when(t == pl.num_programs(1) - 1)
    def _():
        st_ref[...] = full_ref[tt + pad:tt + HIST_ROWS, :]

    full_ref[0:HIST_ROWS, :] = full_ref[tt:tt + HIST_ROWS, :]


def _conv_weight_specs(ch, li, n_grid):
    vec = _layer_spec((1, ch), li, n_grid)
    return [_layer_spec((CONV_K, ch), li, n_grid), vec, vec, vec,
            _whole_spec((ch, ch), n_grid), vec]


def _conv(u, gc, conv_w, cast_w, cast_li, st_prev, li, depth, *, batch, seq, tt):
    m, ch = u.shape
    nt = seq // tt
    tile = pl.BlockSpec((tt, ch), lambda b, t: (b * nt + t, 0))
    casts = [_cast_specs(w, cast_li, batch * nt, lambda b, t: b * nt + t) for w in cast_w]
    in_arrays = [u, gc, *conv_w, *cast_w]
    in_specs = [tile, tile] + _conv_weight_specs(ch, li, 2) + [c[0] for c in casts]
    body, extra, extra_specs, aliases = _alias_args(
        functools.partial(_conv_kernel, n_cast=len(cast_w)), len(in_arrays), st_prev, 1)
    return pl.pallas_call(
        body,
        grid=(batch, nt),
        in_specs=in_specs + extra_specs,
        out_specs=[
            tile,
            pl.BlockSpec((None, None, CONV_K - 1, ch), lambda b, t: (li, b, 0, 0)),
        ] + [c[1] for c in casts],
        out_shape=[
            jax.ShapeDtypeStruct((m, ch), BF16),
            jax.ShapeDtypeStruct((depth, batch, CONV_K - 1, ch), F32),
        ] + [c[2] for c in casts],
        scratch_shapes=[
            pltpu.VMEM((HIST_ROWS + tt, ch), F32),
            pltpu.VMEM((tt, ch), F32),
            pltpu.VMEM((SUBLANES, tt + HIST_ROWS - SUBLANES, ch), F32),
        ],
        input_output_aliases=aliases,
        compiler_params=_params(("arbitrary", "arbitrary")),
        name="conv",
    )(*in_arrays, *extra)


def _conv_step_kernel(u_ref, gc_ref, st_ref, wdw_ref, bdw_ref, gcn_ref, bcn_ref, wpw_ref,
                      bpw_ref, o_ref, st_out_ref):
    hist = CONV_K - 1
    u = u_ref[...]
    c = u * wdw_ref[hist:hist + 1, :] + bdw_ref[...]
    for j in range(hist):
        c = c + st_ref[j] * wdw_ref[j:j + 1, :]
    o_ref[...] = _conv_tail(c, gcn_ref, bcn_ref, wpw_ref, bpw_ref,
                            gc_ref[...].astype(F32))
    for j in range(hist - 1):
        st_out_ref[j] = st_ref[j + 1]
    st_out_ref[hist - 1] = u


def _conv_step(u, gc, state, conv_w, st_prev, li, *, bb):
    n_b, ch = u.shape
    st_spec = pl.BlockSpec((None, CONV_K - 1, bb, ch), lambda i: (li, 0, i, 0))
    tile = pl.BlockSpec((bb, ch), lambda i: (i, 0))
    in_arrays = [u, gc, state, *conv_w]
    in_specs = [tile, tile, st_spec] + _conv_weight_specs(ch, li, 1)
    body, extra, extra_specs, aliases = _alias_args(
        _conv_step_kernel, len(in_arrays), st_prev, 1)
    return pl.pallas_call(
        body,
        grid=(n_b // bb,),
        in_specs=in_specs + extra_specs,
        out_specs=[tile, st_spec],
        out_shape=[
            jax.ShapeDtypeStruct((n_b, ch), BF16),
            jax.ShapeDtypeStruct(state.shape, F32),
        ],
        input_output_aliases=aliases,
        compiler_params=_params(("arbitrary",)),
        name="conv_step",
    )(*in_arrays, *extra)


def _out_tile(ret_ref, conv_ref, x_ref, p_ref, wo_ref, wg_ref, wp_ref, gpost_ref,
              gple_ref, y_ref):
    rw = ret_ref.shape[1]
    m = (jnp.dot(ret_ref[...], wo_ref[0:rw, :], preferred_element_type=F32)
         + jnp.dot(conv_ref[...], wo_ref[rw:, :], preferred_element_type=F32))
    h = x_ref[...] + _rms(m, gpost_ref[...])
    gate = jax.nn.sigmoid(
        jnp.dot(h.astype(BF16), wg_ref[...], preferred_element_type=F32))
    e = jnp.dot(p_ref[...].astype(BF16), wp_ref[...], preferred_element_type=F32)
    y_ref[...] = h + gate * _rms(e, gple_ref[...])


def _out_kernel(retp_ref, convp_ref, xp_ref, pp_ref, rets_ref, convs_ref, xs_ref, ps_ref,
                wo_ref, wg_ref, wp_ref, gpost_ref, gple_ref, yp_ref, ys_ref):
    i = pl.program_id(0)
    last = pl.num_programs(0) - 1
    weights = (wo_ref, wg_ref, wp_ref, gpost_ref, gple_ref)

    @pl.when(i < last)
    def _():
        _out_tile(retp_ref, convp_ref, xp_ref, pp_ref, *weights, yp_ref)

    @pl.when(i == last)
    def _():
        _out_tile(rets_ref, convs_ref, xs_ref, ps_ref, *weights, ys_ref)


def _out_proj(prompt, sample, w_out, w_gate, w_ple, g_post, g_ple, li, *, bm):
    ret_p, conv_p, xp, pp = prompt
    ret_s, conv_s, xs, ps = sample
    m, d = xp.shape
    ms = xs.shape[0]
    pd = pp.shape[2]
    n_tiles = m // bm
    tile = lambda i: jnp.minimum(i, n_tiles - 1)
    rows = lambda width: pl.BlockSpec((bm, width), lambda i: (tile(i), 0))
    whole = lambda width: _whole_spec((ms, width), 1)
    return pl.pallas_call(
        _out_kernel,
        grid=(n_tiles + 1,),
        in_specs=[
            rows(ret_p.shape[1]), rows(conv_p.shape[1]), rows(d),
            pl.BlockSpec((None, bm, pd), lambda i: (li, tile(i), 0)),
            whole(ret_s.shape[1]), whole(conv_s.shape[1]), whole(d),
            pl.BlockSpec((None, ms, pd), lambda i: (li, 0, 0)),
            _whole_spec(w_out.shape, 1, 1), _whole_spec(w_gate.shape, 1, 1),
            _whole_spec(w_ple.shape, 1, 1),
            _layer_spec((1, d), li, 1), _layer_spec((1, d), li, 1),
        ],
        out_specs=[rows(d), whole(d)],
        out_shape=[jax.ShapeDtypeStruct((m, d), F32), jax.ShapeDtypeStruct((ms, d), F32)],
        compiler_params=_params(("arbitrary",)),
        name="out_proj",
    )(ret_p, conv_p, xp, pp, ret_s, conv_s, xs, ps, w_out, w_gate, w_ple, g_post, g_ple)


def _rope_tables(pos):
    half = HEAD_DIM // 2
    inv = ROPE_BASE ** (-jnp.arange(half, dtype=F32) / half)
    ang = pos.astype(F32)[:, None] * inv[None, :]
    cos, sin = jnp.cos(ang), jnp.sin(ang)
    return jnp.concatenate([cos, cos], axis=-1), jnp.concatenate([-sin, sin], axis=-1)


def kernel(x_prompt, x_sample, state_ret, state_conv, p_prompt, p_sample, w_in, w_out,
           g_ret, w_dw, b_dw, g_cn, b_cn, w_pw2, b_pw2, g_pre, g_post, w_ple, g_ple,
           w_ple_gate):
    depth = w_in.shape[0]
    bp, lp, d = x_prompt.shape
    bs, ls, _ = x_sample.shape
    assert ls == 1 and lp % RET_CHUNK == 0

    w_in_b = w_in[0].astype(BF16)

    log_g = jnp.log1p(-jnp.exp2(-5.0 - jnp.arange(RET_HEADS, dtype=F32)))
    lg_tab = jnp.broadcast_to(log_g[:, None, None], (RET_HEADS, 1, RET_CHUNK))
    gam_tab = jnp.broadcast_to(jnp.exp(log_g)[:, None, None], (RET_HEADS, 1, HEAD_DIM))
    rope_p = _rope_tables(jnp.arange(lp, dtype=jnp.int32))
    rope_s = _rope_tables(jnp.full((bs,), PAST_LEN, dtype=jnp.int32))

    vec = lambda a: a.reshape(depth, 1, -1)
    g_pre3, g_post3, g_ple3 = vec(g_pre), vec(g_post), vec(g_ple)
    g_ret4 = g_ret.reshape(depth, RET_HEADS, 1, HEAD_DIM)
    conv_vecs = (vec(b_dw), vec(g_cn), vec(b_cn))
    b_pw3 = vec(b_pw2)
    state_conv_t = jnp.transpose(state_conv, (0, 2, 1, 3))
    pp = p_prompt.reshape(depth, bp * lp, -1)
    ps = p_sample.reshape(depth, bs, -1)

    bm_in, bm_out, tq, tt = 256, 512, 512, 256
    xp = x_prompt.reshape(bp * lp, d)
    xs = x_sample.reshape(bs, d)
    ret_p = conv_p = ret_s = conv_s = None
    for li in range(depth):
        (zr_p, u_p, gc_p, zr_s, u_s, gc_s) = _in_proj(
            xp, xs, g_pre3, w_in_b, rope_p, rope_s, li, bm=bm_in, pos_blocks=lp // bm_in)

        r_p, ret_p, w_out_b, w_gate_b, w_ple_b, w_pw_b = _retention(
            zr_p, g_ret4, lg_tab, (w_out, w_ple_gate, w_ple, w_pw2), ret_p, li, depth,
            batch=bp, seq=lp, tq=tq)
        conv_w = (w_dw, *conv_vecs, w_pw_b, b_pw3)
        next_w = (w_in,) if li + 1 < depth else ()
        c_p, conv_p, *w_in_next = _conv(u_p, gc_p, conv_w, next_w, li + 1, conv_p, li, depth,
                                        batch=bp, seq=lp, tt=tt)
        if w_in_next:
            w_in_b = w_in_next[0]

        r_s, ret_s = _retention_step(zr_s, state_ret, g_ret4, gam_tab, ret_s, li, bb=16)
        c_s, conv_s = _conv_step(u_s, gc_s, state_conv_t, conv_w, conv_s, li, bb=16)

        xp, xs = _out_proj((r_p, c_p, xp, pp), (r_s, c_s, xs, ps),
                           w_out_b, w_gate_b, w_ple_b, g_post3, g_ple3, li, bm=bm_out)

    return (xp.reshape(bp, lp, d), xs.reshape(bs, ls, d), ret_p, conv_p, ret_s,
            jnp.transpose(conv_s, (0, 2, 1, 3)))
```

```python
import functools

import jax
import jax.numpy as jnp
from jax import lax
from jax.experimental import pallas as pl
from jax.experimental.pallas import tpu as pltpu

F32 = jnp.float32
BF16 = jnp.bfloat16

RET_HEADS = 8
HEAD_DIM = 128
RET_WIDTH = RET_HEADS * HEAD_DIM
CONV_K = 31
ROPE_BASE = 10000.0
EPS = 1e-6
PAST_LEN = 16384

VMEM_LIMIT_BYTES = 58 * 1024 * 1024
LANES = 128
SUBLANES = 8
COL_TILE = 1024
RET_CHUNK = 256
CONV_ROW_BLOCK = 64
HIST_ROWS = 32
GATE_CHUNK = 512


def _params(sem):
    return pltpu.CompilerParams(dimension_semantics=sem,
                                vmem_limit_bytes=VMEM_LIMIT_BYTES)


def _silu(x):
    return x * jax.nn.sigmoid(x)


def _rms(x, g):
    return x * lax.rsqrt(jnp.mean(x * x, axis=-1, keepdims=True) + EPS) * g


def _layer_spec(shape, li, n_grid, buffered=None):
    zeros = (0,) * len(shape)
    if n_grid == 1:
        index_map = lambda i: (li,) + zeros
    else:
        index_map = lambda i, j: (li,) + zeros
    if buffered is None:
        return pl.BlockSpec((None,) + tuple(shape), index_map)
    return pl.BlockSpec((None,) + tuple(shape), index_map,
                        pipeline_mode=pl.Buffered(buffered))


def _whole_spec(shape, n_grid, buffered=None):
    zeros = (0,) * len(shape)
    index_map = (lambda i: zeros) if n_grid == 1 else (lambda i, j: zeros)
    if buffered is None:
        return pl.BlockSpec(tuple(shape), index_map)
    return pl.BlockSpec(tuple(shape), index_map, pipeline_mode=pl.Buffered(buffered))


def _cast_specs(w, li, n_steps, step_of):
    _, rows, cols = w.shape
    r = rows // n_steps
    assert r * n_steps == rows and r % 16 == 0
    return (pl.BlockSpec((None, r, cols), lambda *g: (li, step_of(*g), 0)),
            pl.BlockSpec((r, cols), lambda *g: (step_of(*g), 0)),
            jax.ShapeDtypeStruct((rows, cols), BF16))


def _with_alias_slot(body, n_in):
    def wrapped(*refs):
        return body(*refs[:n_in], *refs[n_in + 1:])
    return wrapped


def _alias_args(body, n_in, prev, out_idx):
    if prev is None:
        return body, [], [], {}
    return (_with_alias_slot(body, n_in), [prev],
            [pl.BlockSpec(memory_space=pl.ANY)], {n_in: out_idx})


def _in_proj_tile(x_ref, gpre_ref, w_ref, cos_ref, sin_ref, zr_ref, u_ref, gc_ref):
    hn = _rms(x_ref[...], gpre_ref[...]).astype(BF16)

    def proj(g):
        return jnp.dot(hn, w_ref[:, g * COL_TILE:(g + 1) * COL_TILE],
                       preferred_element_type=F32)

    def rope(z, scale, col0):
        c = cos_ref[...]
        s = sin_ref[...]
        for h in range(RET_HEADS):
            xh = z[:, h * HEAD_DIM:(h + 1) * HEAD_DIM]
            rot = pltpu.roll(xh, HEAD_DIM // 2, axis=1)
            r = xh * c + rot * s
            if scale is not None:
                r = r * scale
            zr_ref[:, col0 + h * HEAD_DIM:col0 + (h + 1) * HEAD_DIM] = r.astype(BF16)

    rope(proj(0), None, 0)
    rope(proj(1), HEAD_DIM ** -0.5, COL_TILE)
    zr_ref[:, 2 * COL_TILE:3 * COL_TILE] = proj(2).astype(BF16)
    zr_ref[:, 3 * COL_TILE:4 * COL_TILE] = _silu(proj(3)).astype(BF16)
    u_ref[...] = proj(4) * jax.nn.sigmoid(proj(5))
    gc_ref[...] = _silu(proj(6)).astype(BF16)


def _in_proj_kernel(xp_ref, xs_ref, gpre_ref, w_ref, cosp_ref, sinp_ref, coss_ref, sins_ref,
                    zrp_ref, up_ref, gcp_ref, zrs_ref, us_ref, gcs_ref):
    i = pl.program_id(0)
    last = pl.num_programs(0) - 1

    @pl.when(i < last)
    def _():
        _in_proj_tile(xp_ref, gpre_ref, w_ref, cosp_ref, sinp_ref, zrp_ref, up_ref, gcp_ref)

    @pl.when(i == last)
    def _():
        _in_proj_tile(xs_ref, gpre_ref, w_ref, coss_ref, sins_ref, zrs_ref, us_ref, gcs_ref)


def _group_out_shapes(m):
    return [
        jax.ShapeDtypeStruct((m, 4 * COL_TILE), BF16),
        jax.ShapeDtypeStruct((m, COL_TILE), F32),
        jax.ShapeDtypeStruct((m, COL_TILE), BF16),
    ]


def _in_proj(xp, xs, g_pre, w_in, rope_p, rope_s, li, *, bm, pos_blocks):
    m, d = xp.shape
    ms = xs.shape[0]
    n_cols = w_in.shape[1]
    n_tiles = m // bm
    assert n_cols == 7 * COL_TILE and m % bm == 0
    tile = lambda i: jnp.minimum(i, n_tiles - 1)
    rows = lambda width: pl.BlockSpec((bm, width), lambda i: (tile(i), 0))
    pos = pl.BlockSpec((bm, HEAD_DIM), lambda i: (tile(i) % pos_blocks, 0))
    whole = lambda width: _whole_spec((ms, width), 1)
    return pl.pallas_call(
        _in_proj_kernel,
        grid=(n_tiles + 1,),
        in_specs=[rows(d), whole(d), _layer_spec((1, d), li, 1),
                  _whole_spec((d, n_cols), 1, 1), pos, pos,
                  whole(HEAD_DIM), whole(HEAD_DIM)],
        out_specs=[rows(4 * COL_TILE), rows(COL_TILE), rows(COL_TILE),
                   whole(4 * COL_TILE), whole(COL_TILE), whole(COL_TILE)],
        out_shape=_group_out_shapes(m) + _group_out_shapes(ms),
        compiler_params=_params(("arbitrary",)),
        name="in_proj",
    )(xp, xs, g_pre, w_in, *rope_p, *rope_s)


def _group_norm_gate(o, g_ret, gate):
    mu = jnp.mean(o, axis=-1, keepdims=True)
    d = o - mu
    var = jnp.mean(d * d, axis=-1, keepdims=True)
    return (d * lax.rsqrt(var + EPS)) * g_ret * gate


def _retention_kernel(q_ref, k_ref, v_ref, gr_ref, gret_ref, lg_ref, *rest, n_cast):
    cast_in, rest = rest[:n_cast], rest[n_cast:]
    o_ref, s_out_ref = rest[:2]
    cast_out = rest[2:2 + n_cast]
    s_ref, decay_ref, cross_ref, kdec_ref = rest[2 + n_cast:]
    for src, dst in zip(cast_in, cast_out):
        dst[...] = src[...].astype(BF16)

    c = RET_CHUNK
    b = pl.program_id(0)
    t = pl.program_id(1)

    @pl.when((b == 0) & (t == 0))
    def _():
        row = lax.broadcasted_iota(jnp.int32, (c, c), 0)
        col = lax.broadcasted_iota(jnp.int32, (c, c), 1)
        rel = (row - col).astype(F32)
        idx = lax.broadcasted_iota(jnp.int32, (c, LANES), 0).astype(F32)
        for h in range(RET_HEADS):
            lg = lg_ref[h]
            decay_ref[h] = jnp.where(rel >= 0, jnp.exp(lg * jnp.maximum(rel, 0.0)), 0.0)
            cross_ref[h] = jnp.exp(lg[:, :LANES] * (idx + 1.0))
            kdec_ref[h] = jnp.exp(lg[:, :LANES] * (c - 1.0 - idx))

    @pl.when(t == 0)
    def _():
        s_ref[...] = jnp.zeros(s_ref.shape, F32)

    for ci in range(q_ref.shape[0] // c):
        rows = slice(ci * c, (ci + 1) * c)
        for h in range(RET_HEADS):
            cols = slice(h * HEAD_DIM, (h + 1) * HEAD_DIM)
            q = q_ref[rows, cols]
            k = k_ref[rows, cols]
            v = v_ref[rows, cols]
            s = s_ref[h]
            g_chunk = jnp.exp(lg_ref[h][:, :LANES] * float(c))
            scores = lax.dot_general(q, k, (((1,), (1,)), ((), ())),
                                     preferred_element_type=F32) * decay_ref[h]
            o = jnp.dot(scores.astype(BF16), v, preferred_element_type=F32)
            o = o + jnp.dot(q, s.astype(BF16), preferred_element_type=F32) * cross_ref[h]
            kd = (k.astype(F32) * kdec_ref[h]).astype(BF16)
            s_ref[h] = g_chunk * s + lax.dot_general(
                kd, v, (((0,), (0,)), ((), ())), preferred_element_type=F32)
            o_ref[rows, cols] = _group_norm_gate(
                o, gret_ref[h], gr_ref[rows, cols].astype(F32)).astype(BF16)

    @pl.when(t == pl.num_programs(1) - 1)
    def _():
        s_out_ref[...] = s_ref[...]


def _retention(zr, g_ret, lg_tab, cast_w, s_prev, li, depth, *, batch, seq, tq):
    m = zr.shape[0]
    nt = seq // tq
    blk = lambda g: pl.BlockSpec((tq, RET_WIDTH), lambda b, t: (b * nt + t, g))
    casts = [_cast_specs(w, li, batch * nt, lambda b, t: b * nt + t) for w in cast_w]
    in_arrays = [zr, zr, zr, zr, g_ret, lg_tab, *cast_w]
    in_specs = [blk(0), blk(1), blk(2), blk(3),
                _layer_spec((RET_HEADS, 1, HEAD_DIM), li, 2),
                pl.BlockSpec(lg_tab.shape, lambda b, t: (0, 0, 0))] + [c[0] for c in casts]
    body, extra, extra_specs, aliases = _alias_args(
        functools.partial(_retention_kernel, n_cast=len(cast_w)),
        len(in_arrays), s_prev, 1)
    return pl.pallas_call(
        body,
        grid=(batch, nt),
        in_specs=in_specs + extra_specs,
        out_specs=[
            pl.BlockSpec((tq, RET_WIDTH), lambda b, t: (b * nt + t, 0)),
            pl.BlockSpec((None, None, RET_HEADS, HEAD_DIM, HEAD_DIM),
                         lambda b, t: (li, b, 0, 0, 0)),
        ] + [c[1] for c in casts],
        out_shape=[
            jax.ShapeDtypeStruct((m, RET_WIDTH), BF16),
            jax.ShapeDtypeStruct((depth, batch, RET_HEADS, HEAD_DIM, HEAD_DIM), F32),
        ] + [c[2] for c in casts],
        scratch_shapes=[
            pltpu.VMEM((RET_HEADS, HEAD_DIM, HEAD_DIM), F32),
            pltpu.VMEM((RET_HEADS, RET_CHUNK, RET_CHUNK), F32),
            pltpu.VMEM((RET_HEADS, RET_CHUNK, LANES), F32),
            pltpu.VMEM((RET_HEADS, RET_CHUNK, LANES), F32),
        ],
        input_output_aliases=aliases,
        compiler_params=_params(("arbitrary", "arbitrary")),
        name="retention",
    )(*in_arrays, *extra)


def _retention_step_kernel(zr_ref, s_ref, gret_ref, gam_ref, o_ref, s_out_ref, oc_ref,
                           *, bb):
    i = pl.program_id(0)
    n_b = zr_ref.shape[0]
    rows = pl.ds(pl.multiple_of(i * bb, bb), bb)
    shift = (n_b - i * bb) % n_b
    for h in range(RET_HEADS):
        hs = lambda g: slice((g * RET_HEADS + h) * HEAD_DIM,
                             (g * RET_HEADS + h + 1) * HEAD_DIM)
        q_t = pltpu.roll(zr_ref[:, hs(0)].astype(F32).T, shift, axis=1)
        k_t = pltpu.roll(zr_ref[:, hs(1)].astype(F32).T, shift, axis=1)
        q = zr_ref[rows, hs(0)].astype(F32)
        k = zr_ref[rows, hs(1)].astype(F32)
        v = zr_ref[rows, hs(2)].astype(F32)
        gate = zr_ref[rows, hs(3)].astype(F32)
        gam = gam_ref[h]
        for b in range(bb):
            s = s_ref[b, h]
            oc_ref[b:b + 1, :] = jnp.sum(q_t[:, b:b + 1] * s, axis=0, keepdims=True)
            s_out_ref[b, h] = gam * s + k_t[:, b:b + 1] * v[b:b + 1, :]
        o = jnp.sum(q * k, axis=-1, keepdims=True) * v + oc_ref[...] * gam
        o_ref[:, hs(0)] = _group_norm_gate(o, gret_ref[h], gate).astype(BF16)


def _retention_step(zr, state, g_ret, gam_tab, s_prev, li, *, bb):
    n_b = zr.shape[0]
    st_spec = pl.BlockSpec((None, bb, RET_HEADS, HEAD_DIM, HEAD_DIM),
                           lambda i: (li, i, 0, 0, 0))
    in_arrays = [zr, state, g_ret, gam_tab]
    in_specs = [pl.BlockSpec(zr.shape, lambda i: (0, 0)), st_spec,
                _layer_spec((RET_HEADS, 1, HEAD_DIM), li, 1),
                pl.BlockSpec(gam_tab.shape, lambda i: (0, 0, 0))]
    body, extra, extra_specs, aliases = _alias_args(
        functools.partial(_retention_step_kernel, bb=bb), len(in_arrays), s_prev, 1)
    return pl.pallas_call(
        body,
        grid=(n_b // bb,),
        in_specs=in_specs + extra_specs,
        out_specs=[pl.BlockSpec((bb, RET_WIDTH), lambda i: (i, 0)), st_spec],
        out_shape=[
            jax.ShapeDtypeStruct((n_b, RET_WIDTH), BF16),
            jax.ShapeDtypeStruct(state.shape, F32),
        ],
        scratch_shapes=[pltpu.VMEM((bb, HEAD_DIM), F32)],
        input_output_aliases=aliases,
        compiler_params=_params(("arbitrary",)),
        name="retention_step",
    )(*in_arrays, *extra)


def _conv_tail(c, gcn_ref, bcn_ref, wpw_ref, bpw_ref, gate):
    mu = jnp.mean(c, axis=-1, keepdims=True)
    d = c - mu
    var = jnp.mean(d * d, axis=-1, keepdims=True)
    y = _silu(d * lax.rsqrt(var + EPS) * gcn_ref[...] + bcn_ref[...])
    y = jnp.dot(y.astype(BF16), wpw_ref[...], preferred_element_type=F32) + bpw_ref[...]
    return (y * gate).astype(BF16)


def _conv_kernel(u_ref, gc_ref, wdw_ref, bdw_ref, gcn_ref, bcn_ref, wpw_ref, bpw_ref,
                 *rest, n_cast):
    cast_in, rest = rest[:n_cast], rest[n_cast:]
    o_ref, st_ref = rest[:2]
    cast_out = rest[2:2 + n_cast]
    full_ref, c_ref, sh_ref = rest[2 + n_cast:]
    for src, dst in zip(cast_in, cast_out):
        dst[...] = src[...].astype(BF16)

    t = pl.program_id(1)
    tt, ch = u_ref.shape
    pad = HIST_ROWS - (CONV_K - 1)

    @pl.when(t == 0)
    def _():
        full_ref[0:HIST_ROWS, :] = jnp.zeros((HIST_ROWS, ch), F32)

    full_ref[HIST_ROWS:HIST_ROWS + tt, :] = u_ref[...]

    for p in range(SUBLANES):
        span = tt + SUBLANES * ((CONV_K - 1 - p) // SUBLANES)
        sh_ref[p, 0:span, :] = full_ref[pad + p:pad + p + span, :]

    for r0 in range(0, tt, CONV_ROW_BLOCK):
        for cb in range(ch // LANES):
            cs = slice(cb * LANES, (cb + 1) * LANES)
            acc = bdw_ref[:, cs]
            for j in range(CONV_K):
                p, a = j % SUBLANES, j // SUBLANES
                row = r0 + a * SUBLANES
                acc = acc + sh_ref[p, row:row + CONV_ROW_BLOCK, cs] * wdw_ref[j:j + 1, cs]
            c_ref[r0:r0 + CONV_ROW_BLOCK, cs] = acc

    o_ref[...] = _conv_tail(c_ref[...], gcn_ref, bcn_ref, wpw_ref, bpw_ref,
                            gc_ref[...].astype(F32))

    @pl.when(t == pl.num_programs(1) - 1)
    def _():
        st_ref[...] = full_ref[tt + pad:tt + HIST_ROWS, :]

    full_ref[0:HIST_ROWS, :] = full_ref[tt:tt + HIST_ROWS, :]


def _conv_weight_specs(ch, li, n_grid):
    vec = _layer_spec((1, ch), li, n_grid)
    return [_layer_spec((CONV_K, ch), li, n_grid), vec, vec, vec,
            _whole_spec((ch, ch), n_grid), vec]


def _conv(u, gc, conv_w, cast_w, cast_li, st_prev, li, depth, *, batch, seq, tt):
    m, ch = u.shape
    nt = seq // tt
    tile = pl.BlockSpec((tt, ch), lambda b, t: (b * nt + t, 0))
    casts = [_cast_specs(w, cast_li, batch * nt, lambda b, t: b * nt + t) for w in cast_w]
    in_arrays = [u, gc, *conv_w, *cast_w]
    in_specs = [tile, tile] + _conv_weight_specs(ch, li, 2) + [c[0] for c in casts]
    body, extra, extra_specs, aliases = _alias_args(
        functools.partial(_conv_kernel, n_cast=len(cast_w)), len(in_arrays), st_prev, 1)
    return pl.pallas_call(
        body,
        grid=(batch, nt),
        in_specs=in_specs + extra_specs,
        out_specs=[
            tile,
            pl.BlockSpec((None, None, CONV_K - 1, ch), lambda b, t: (li, b, 0, 0)),
        ] + [c[1] for c in casts],
        out_shape=[
            jax.ShapeDtypeStruct((m, ch), BF16),
            jax.ShapeDtypeStruct((depth, batch, CONV_K - 1, ch), F32),
        ] + [c[2] for c in casts],
        scratch_shapes=[
            pltpu.VMEM((HIST_ROWS + tt, ch), F32),
            pltpu.VMEM((tt, ch), F32),
            pltpu.VMEM((SUBLANES, tt + HIST_ROWS - SUBLANES, ch), F32),
        ],
        input_output_aliases=aliases,
        compiler_params=_params(("arbitrary", "arbitrary")),
        name="conv",
    )(*in_arrays, *extra)


def _conv_step_kernel(u_ref, gc_ref, st_ref, wdw_ref, bdw_ref, gcn_ref, bcn_ref, wpw_ref,
                      bpw_ref, o_ref, st_out_ref):
    hist = CONV_K - 1
    u = u_ref[...]
    c = u * wdw_ref[hist:hist + 1, :] + bdw_ref[...]
    for j in range(hist):
        c = c + st_ref[j] * wdw_ref[j:j + 1, :]
    o_ref[...] = _conv_tail(c, gcn_ref, bcn_ref, wpw_ref, bpw_ref,
                            gc_ref[...].astype(F32))
    for j in range(hist - 1):
        st_out_ref[j] = st_ref[j + 1]
    st_out_ref[hist - 1] = u


def _conv_step(u, gc, state, conv_w, st_prev, li, *, bb):
    n_b, ch = u.shape
    st_spec = pl.BlockSpec((None, CONV_K - 1, bb, ch), lambda i: (li, 0, i, 0))
    tile = pl.BlockSpec((bb, ch), lambda i: (i, 0))
    in_arrays = [u, gc, state, *conv_w]
    in_specs = [tile, tile, st_spec] + _conv_weight_specs(ch, li, 1)
    body, extra, extra_specs, aliases = _alias_args(
        _conv_step_kernel, len(in_arrays), st_prev, 1)
    return pl.pallas_call(
        body,
        grid=(n_b // bb,),
        in_specs=in_specs + extra_specs,
        out_specs=[tile, st_spec],
        out_shape=[
            jax.ShapeDtypeStruct((n_b, ch), BF16),
            jax.ShapeDtypeStruct(state.shape, F32),
        ],
        input_output_aliases=aliases,
        compiler_params=_params(("arbitrary",)),
        name="conv_step",
    )(*in_arrays, *extra)


def _out_tile(ret_ref, conv_ref, x_ref, p_ref, wo_ref, wg_ref, wp_ref, gpost_ref,
              gple_ref, y_ref):
    rw = ret_ref.shape[1]
    m = (jnp.dot(ret_ref[...], wo_ref[0:rw, :], preferred_element_type=F32)
         + jnp.dot(conv_ref[...], wo_ref[rw:, :], preferred_element_type=F32))
    h = x_ref[...] + _rms(m, gpost_ref[...])
    h_b = h.astype(BF16)
    e = jnp.dot(p_ref[...].astype(BF16), wp_ref[...], preferred_element_type=F32)
    e = _rms(e, gple_ref[...])
    d = h.shape[1]
    for c0 in range(0, d, GATE_CHUNK):
        cols = slice(c0, c0 + GATE_CHUNK)
        gate = jax.nn.sigmoid(jnp.dot(h_b, wg_ref[:, cols], preferred_element_type=F32))
        y_ref[:, cols] = h[:, cols] + gate * e[:, cols]


def _out_kernel(retp_ref, convp_ref, xp_ref, pp_ref, rets_ref, convs_ref, xs_ref, ps_ref,
                wo_ref, wg_ref, wp_ref, gpost_ref, gple_ref, yp_ref, ys_ref):
    i = pl.program_id(0)
    last = pl.num_programs(0) - 1
    weights = (wo_ref, wg_ref, wp_ref, gpost_ref, gple_ref)

    @pl.when(i < last)
    def _():
        _out_tile(retp_ref, convp_ref, xp_ref, pp_ref, *weights, yp_ref)

    @pl.when(i == last)
    def _():
        _out_tile(rets_ref, convs_ref, xs_ref, ps_ref, *weights, ys_ref)


def _out_proj(prompt, sample, w_out, w_gate, w_ple, g_post, g_ple, li, *, bm):
    ret_p, conv_p, xp, pp = prompt
    ret_s, conv_s, xs, ps = sample
    m, d = xp.shape
    ms = xs.shape[0]
    pd = pp.shape[2]
    n_tiles = m // bm
    tile = lambda i: jnp.minimum(i, n_tiles - 1)
    rows = lambda width: pl.BlockSpec((bm, width), lambda i: (tile(i), 0))
    whole = lambda width: _whole_spec((ms, width), 1)
    return pl.pallas_call(
        _out_kernel,
        grid=(n_tiles + 1,),
        in_specs=[
            rows(ret_p.shape[1]), rows(conv_p.shape[1]), rows(d),
            pl.BlockSpec((None, bm, pd), lambda i: (li, tile(i), 0)),
            whole(ret_s.shape[1]), whole(conv_s.shape[1]), whole(d),
            pl.BlockSpec((None, ms, pd), lambda i: (li, 0, 0)),
            _whole_spec(w_out.shape, 1, 1), _whole_spec(w_gate.shape, 1, 1),
            _whole_spec(w_ple.shape, 1, 1),
            _layer_spec((1, d), li, 1), _layer_spec((1, d), li, 1),
        ],
        out_specs=[rows(d), whole(d)],
        out_shape=[jax.ShapeDtypeStruct((m, d), F32), jax.ShapeDtypeStruct((ms, d), F32)],
        compiler_params=_params(("arbitrary",)),
        name="out_proj",
    )(ret_p, conv_p, xp, pp, ret_s, conv_s, xs, ps, w_out, w_gate, w_ple, g_post, g_ple)


def _rope_tables(pos):
    half = HEAD_DIM // 2
    inv = ROPE_BASE ** (-jnp.arange(half, dtype=F32) / half)
    ang = pos.astype(F32)[:, None] * inv[None, :]
    cos, sin = jnp.cos(ang), jnp.sin(ang)
    return jnp.concatenate([cos, cos], axis=-1), jnp.concatenate([-sin, sin], axis=-1)


def kernel(x_prompt, x_sample, state_ret, state_conv, p_prompt, p_sample, w_in, w_out,
           g_ret, w_dw, b_dw, g_cn, b_cn, w_pw2, b_pw2, g_pre, g_post, w_ple, g_ple,
           w_ple_gate):
    depth = w_in.shape[0]
    bp, lp, d = x_prompt.shape
    bs, ls, _ = x_sample.shape
    assert ls == 1 and lp % RET_CHUNK == 0

    w_in_b = w_in[0].astype(BF16)

    log_g = jnp.log1p(-jnp.exp2(-5.0 - jnp.arange(RET_HEADS, dtype=F32)))
    lg_tab = jnp.broadcast_to(log_g[:, None, None], (RET_HEADS, 1, RET_CHUNK))
    gam_tab = jnp.broadcast_to(jnp.exp(log_g)[:, None, None], (RET_HEADS, 1, HEAD_DIM))
    rope_p = _rope_tables(jnp.arange(lp, dtype=jnp.int32))
    rope_s = _rope_tables(jnp.full((bs,), PAST_LEN, dtype=jnp.int32))

    vec = lambda a: a.reshape(depth, 1, -1)
    g_pre3, g_post3, g_ple3 = vec(g_pre), vec(g_post), vec(g_ple)
    g_ret4 = g_ret.reshape(depth, RET_HEADS, 1, HEAD_DIM)
    conv_vecs = (vec(b_dw), vec(g_cn), vec(b_cn))
    b_pw3 = vec(b_pw2)
    state_conv_t = jnp.transpose(state_conv, (0, 2, 1, 3))
    pp = p_prompt.reshape(depth, bp * lp, -1)
    ps = p_sample.reshape(depth, bs, -1)

    bm_in, bm_out, tq, tt = 256, 512, 512, 256
    xp = x_prompt.reshape(bp * lp, d)
    xs = x_sample.reshape(bs, d)
    ret_p = conv_p = ret_s = conv_s = None
    for li in range(depth):
        (zr_p, u_p, gc_p, zr_s, u_s, gc_s) = _in_proj(
            xp, xs, g_pre3, w_in_b, rope_p, rope_s, li, bm=bm_in, pos_blocks=lp // bm_in)

        if li == 0:
            r_p, ret_p, w_ple_b, w_out_b, w_gate_b, w_pw_b = _retention(
                zr_p, g_ret4, lg_tab, (w_ple, w_out, w_ple_gate, w_pw2), ret_p, li, depth,
                batch=bp, seq=lp, tq=tq)
        else:
            r_p, ret_p, w_ple_b = _retention(zr_p, g_ret4, lg_tab, (w_ple,), ret_p, li,
                                             depth, batch=bp, seq=lp, tq=tq)
        conv_w = (w_dw, *conv_vecs, w_pw_b, b_pw3)
        out_w = (w_out_b, w_gate_b, w_ple_b)
        next_w = (w_in, w_out, w_ple_gate, w_pw2) if li + 1 < depth else ()
        c_p, conv_p, *next_b = _conv(u_p, gc_p, conv_w, next_w, li + 1, conv_p, li, depth,
                                     batch=bp, seq=lp, tt=tt)
        if next_b:
            w_in_b, w_out_b, w_gate_b, w_pw_b = next_b

        r_s, ret_s = _retention_step(zr_s, state_ret, g_ret4, gam_tab, ret_s, li, bb=16)
        c_s, conv_s = _conv_step(u_s, gc_s, state_conv_t, conv_w, conv_s, li, bb=16)

        xp, xs = _out_proj((r_p, c_p, xp, pp), (r_s, c_s, xs, ps),
                           *out_w, g_post3, g_ple3, li, bm=bm_out)

    return (xp.reshape(bp, lp, d), xs.reshape(bs, ls, d), ret_p, conv_p, ret_s,
            jnp.transpose(conv_s, (0, 2, 1, 3)))
```

```python
import functools

import jax
import jax.numpy as jnp
from jax import lax
from jax.experimental import pallas as pl
from jax.experimental.pallas import tpu as pltpu

F32 = jnp.float32
BF16 = jnp.bfloat16

RET_HEADS = 8
HEAD_DIM = 128
RET_WIDTH = RET_HEADS * HEAD_DIM
CONV_K = 31
ROPE_BASE = 10000.0
EPS = 1e-6
PAST_LEN = 16384

VMEM_LIMIT_BYTES = 58 * 1024 * 1024
LANES = 128
SUBLANES = 8
COL_TILE = 1024
RET_CHUNK = 256
CONV_ROW_BLOCK = 64
HIST_ROWS = 32
GATE_CHUNK = 512


def _params(sem):
    return pltpu.CompilerParams(dimension_semantics=sem,
                                vmem_limit_bytes=VMEM_LIMIT_BYTES)


def _silu(x):
    return x * jax.nn.sigmoid(x)


def _rms(x, g):
    return x * lax.rsqrt(jnp.mean(x * x, axis=-1, keepdims=True) + EPS) * g


def _layer_spec(shape, li, n_grid, buffered=None):
    zeros = (0,) * len(shape)
    if n_grid == 1:
        index_map = lambda i: (li,) + zeros
    else:
        index_map = lambda i, j: (li,) + zeros
    if buffered is None:
        return pl.BlockSpec((None,) + tuple(shape), index_map)
    return pl.BlockSpec((None,) + tuple(shape), index_map,
                        pipeline_mode=pl.Buffered(buffered))


def _whole_spec(shape, n_grid, buffered=None):
    zeros = (0,) * len(shape)
    index_map = (lambda i: zeros) if n_grid == 1 else (lambda i, j: zeros)
    if buffered is None:
        return pl.BlockSpec(tuple(shape), index_map)
    return pl.BlockSpec(tuple(shape), index_map, pipeline_mode=pl.Buffered(buffered))


def _cast_specs(w, li, n_steps, step_of):
    _, rows, cols = w.shape
    r = rows // n_steps
    assert r * n_steps == rows and r % 16 == 0
    return (pl.BlockSpec((None, r, cols), lambda *g: (li, step_of(*g), 0)),
            pl.BlockSpec((r, cols), lambda *g: (step_of(*g), 0)),
            jax.ShapeDtypeStruct((rows, cols), BF16))


def _with_alias_slot(body, n_in):
    def wrapped(*refs):
        return body(*refs[:n_in], *refs[n_in + 1:])
    return wrapped


def _alias_args(body, n_in, prev, out_idx):
    if prev is None:
        return body, [], [], {}
    return (_with_alias_slot(body, n_in), [prev],
            [pl.BlockSpec(memory_space=pl.ANY)], {n_in: out_idx})


def _in_proj_tile(x_ref, gpre_ref, w_ref, cos_ref, sin_ref, zr_ref, u_ref, gc_ref,
                  wait_group=None):
    hn = _rms(x_ref[...], gpre_ref[...]).astype(BF16)

    def proj(g):
        if wait_group is not None:
            wait_group(g)
        return jnp.dot(hn, w_ref[:, g * COL_TILE:(g + 1) * COL_TILE],
                       preferred_element_type=F32)

    def rope(z, scale, col0):
        c = cos_ref[...]
        s = sin_ref[...]
        for h in range(RET_HEADS):
            xh = z[:, h * HEAD_DIM:(h + 1) * HEAD_DIM]
            rot = pltpu.roll(xh, HEAD_DIM // 2, axis=1)
            r = xh * c + rot * s
            if scale is not None:
                r = r * scale
            zr_ref[:, col0 + h * HEAD_DIM:col0 + (h + 1) * HEAD_DIM] = r.astype(BF16)

    rope(proj(0), None, 0)
    rope(proj(1), HEAD_DIM ** -0.5, COL_TILE)
    zr_ref[:, 3 * COL_TILE:4 * COL_TILE] = _silu(proj(3)).astype(BF16)
    u_ref[...] = proj(4) * jax.nn.sigmoid(proj(5))
    gc_ref[...] = _silu(proj(6)).astype(BF16)
    zr_ref[:, 2 * COL_TILE:3 * COL_TILE] = proj(2).astype(BF16)


def _in_proj_kernel(xp_ref, xs_ref, gpre_ref, w_hbm_ref, cosp_ref, sinp_ref, coss_ref,
                    sins_ref, zrp_ref, up_ref, gcp_ref, zrs_ref, us_ref, gcs_ref,
                    w_ref, w_sem):
    i = pl.program_id(0)
    last = pl.num_programs(0) - 1
    n_groups = w_ref.shape[1] // COL_TILE
    prompt = (xp_ref, gpre_ref, w_ref, cosp_ref, sinp_ref, zrp_ref, up_ref, gcp_ref)

    def group_copy(g):
        cols = pl.ds(g * COL_TILE, COL_TILE)
        return pltpu.make_async_copy(w_hbm_ref.at[:, cols], w_ref.at[:, cols], w_sem.at[g])

    @pl.when(i == 0)
    def _():
        for g in range(n_groups):
            group_copy(g).start()
        _in_proj_tile(*prompt, wait_group=lambda g: group_copy(g).wait())

    @pl.when((i > 0) & (i < last))
    def _():
        _in_proj_tile(*prompt)

    @pl.when(i == last)
    def _():
        _in_proj_tile(xs_ref, gpre_ref, w_ref, coss_ref, sins_ref, zrs_ref, us_ref, gcs_ref)


def _group_out_shapes(m):
    return [
        jax.ShapeDtypeStruct((m, 4 * COL_TILE), BF16),
        jax.ShapeDtypeStruct((m, COL_TILE), F32),
        jax.ShapeDtypeStruct((m, COL_TILE), BF16),
    ]


def _in_proj(xp, xs, g_pre, w_in, rope_p, rope_s, li, *, bm, pos_blocks):
    m, d = xp.shape
    ms = xs.shape[0]
    n_cols = w_in.shape[1]
    n_tiles = m // bm
    assert n_cols == 7 * COL_TILE and m % bm == 0
    tile = lambda i: jnp.minimum(i, n_tiles - 1)
    rows = lambda width: pl.BlockSpec((bm, width), lambda i: (tile(i), 0))
    pos = pl.BlockSpec((bm, HEAD_DIM), lambda i: (tile(i) % pos_blocks, 0))
    whole = lambda width: _whole_spec((ms, width), 1)
    return pl.pallas_call(
        _in_proj_kernel,
        grid=(n_tiles + 1,),
        in_specs=[rows(d), whole(d), _layer_spec((1, d), li, 1),
                  pl.BlockSpec(memory_space=pl.ANY), pos, pos,
                  whole(HEAD_DIM), whole(HEAD_DIM)],
        out_specs=[rows(4 * COL_TILE), rows(COL_TILE), rows(COL_TILE),
                   whole(4 * COL_TILE), whole(COL_TILE), whole(COL_TILE)],
        out_shape=_group_out_shapes(m) + _group_out_shapes(ms),
        scratch_shapes=[pltpu.VMEM((d, n_cols), BF16),
                        pltpu.SemaphoreType.DMA((n_cols // COL_TILE,))],
        compiler_params=_params(("arbitrary",)),
        name="in_proj",
    )(xp, xs, g_pre, w_in, *rope_p, *rope_s)


def _group_norm_gate(o, g_ret, gate):
    mu = jnp.mean(o, axis=-1, keepdims=True)
    d = o - mu
    var = jnp.mean(d * d, axis=-1, keepdims=True)
    return (d * lax.rsqrt(var + EPS)) * g_ret * gate


def _retention_kernel(q_ref, k_ref, v_ref, gr_ref, gret_ref, lg_ref, *rest, n_cast):
    cast_in, rest = rest[:n_cast], rest[n_cast:]
    o_ref, s_out_ref = rest[:2]
    cast_out = rest[2:2 + n_cast]
    s_ref, decay_ref, cross_ref, kdec_ref = rest[2 + n_cast:]
    for src, dst in zip(cast_in, cast_out):
        dst[...] = src[...].astype(BF16)

    c = RET_CHUNK
    t = pl.program_id(0)

    @pl.when(t == 0)
    def _():
        row = lax.broadcasted_iota(jnp.int32, (c, c), 0)
        col = lax.broadcasted_iota(jnp.int32, (c, c), 1)
        rel = (row - col).astype(F32)
        idx = lax.broadcasted_iota(jnp.int32, (c, LANES), 0).astype(F32)
        for h in range(RET_HEADS):
            lg = lg_ref[h]
            decay_ref[h] = jnp.where(rel >= 0, jnp.exp(lg * jnp.maximum(rel, 0.0)), 0.0)
            cross_ref[h] = jnp.exp(lg[:, :LANES] * (idx + 1.0))
            kdec_ref[h] = jnp.exp(lg[:, :LANES] * (c - 1.0 - idx))
        s_ref[...] = jnp.zeros(s_ref.shape, F32)

    for b in range(q_ref.shape[0]):
        for h in range(RET_HEADS):
            cols = slice(h * HEAD_DIM, (h + 1) * HEAD_DIM)
            q = q_ref[b, :, cols]
            k = k_ref[b, :, cols]
            v = v_ref[b, :, cols]
            s = s_ref[b, h]
            g_chunk = jnp.exp(lg_ref[h][:, :LANES] * float(c))
            scores = lax.dot_general(q, k, (((1,), (1,)), ((), ())),
                                     preferred_element_type=F32) * decay_ref[h]
            o = jnp.dot(scores.astype(BF16), v, preferred_element_type=F32)
            o = o + jnp.dot(q, s.astype(BF16), preferred_element_type=F32) * cross_ref[h]
            kd = (k.astype(F32) * kdec_ref[h]).astype(BF16)
            s_ref[b, h] = g_chunk * s + lax.dot_general(
                kd, v, (((0,), (0,)), ((), ())), preferred_element_type=F32)
            o_ref[b, :, cols] = _group_norm_gate(
                o, gret_ref[h], gr_ref[b, :, cols].astype(F32)).astype(BF16)

    @pl.when(t == pl.num_programs(0) - 1)
    def _():
        s_out_ref[...] = s_ref[...]


def _retention(zr, g_ret, lg_tab, cast_w, s_prev, li, depth, *, batch, seq):
    m = zr.shape[0]
    nt = seq // RET_CHUNK
    zr3 = zr.reshape(batch, seq, zr.shape[1])
    blk = lambda g: pl.BlockSpec((batch, RET_CHUNK, RET_WIDTH), lambda t: (0, t, g))
    casts = [_cast_specs(w, li, nt, lambda t: t) for w in cast_w]
    in_arrays = [zr3, zr3, zr3, zr3, g_ret, lg_tab, *cast_w]
    in_specs = [blk(0), blk(1), blk(2), blk(3),
                _layer_spec((RET_HEADS, 1, HEAD_DIM), li, 1),
                pl.BlockSpec(lg_tab.shape, lambda t: (0, 0, 0))] + [c[0] for c in casts]
    body, extra, extra_specs, aliases = _alias_args(
        functools.partial(_retention_kernel, n_cast=len(cast_w)),
        len(in_arrays), s_prev, 1)
    o, *rest = pl.pallas_call(
        body,
        grid=(nt,),
        in_specs=in_specs + extra_specs,
        out_specs=[
            blk(0),
            pl.BlockSpec((None, batch, RET_HEADS, HEAD_DIM, HEAD_DIM),
                         lambda t: (li, 0, 0, 0, 0)),
        ] + [c[1] for c in casts],
        out_shape=[
            jax.ShapeDtypeStruct((batch, seq, RET_WIDTH), BF16),
            jax.ShapeDtypeStruct((depth, batch, RET_HEADS, HEAD_DIM, HEAD_DIM), F32),
        ] + [c[2] for c in casts],
        scratch_shapes=[
            pltpu.VMEM((batch, RET_HEADS, HEAD_DIM, HEAD_DIM), F32),
            pltpu.VMEM((RET_HEADS, RET_CHUNK, RET_CHUNK), F32),
            pltpu.VMEM((RET_HEADS, RET_CHUNK, LANES), F32),
            pltpu.VMEM((RET_HEADS, RET_CHUNK, LANES), F32),
        ],
        input_output_aliases=aliases,
        compiler_params=_params(("arbitrary",)),
        name="retention",
    )(*in_arrays, *extra)
    return (o.reshape(m, RET_WIDTH), *rest)


def _retention_step_kernel(zr_ref, s_ref, gret_ref, gam_ref, o_ref, s_out_ref, oc_ref,
                           *, bb):
    i = pl.program_id(0)
    n_b = zr_ref.shape[0]
    rows = pl.ds(pl.multiple_of(i * bb, bb), bb)
    shift = (n_b - i * bb) % n_b
    for h in range(RET_HEADS):
        hs = lambda g: slice((g * RET_HEADS + h) * HEAD_DIM,
                             (g * RET_HEADS + h + 1) * HEAD_DIM)
        q_t = pltpu.roll(zr_ref[:, hs(0)].astype(F32).T, shift, axis=1)
        k_t = pltpu.roll(zr_ref[:, hs(1)].astype(F32).T, shift, axis=1)
        q = zr_ref[rows, hs(0)].astype(F32)
        k = zr_ref[rows, hs(1)].astype(F32)
        v = zr_ref[rows, hs(2)].astype(F32)
        gate = zr_ref[rows, hs(3)].astype(F32)
        gam = gam_ref[h]
        for b in range(bb):
            s = s_ref[b, h]
            oc_ref[b:b + 1, :] = jnp.sum(q_t[:, b:b + 1] * s, axis=0, keepdims=True)
            s_out_ref[b, h] = gam * s + k_t[:, b:b + 1] * v[b:b + 1, :]
        o = jnp.sum(q * k, axis=-1, keepdims=True) * v + oc_ref[...] * gam
        o_ref[:, hs(0)] = _group_norm_gate(o, gret_ref[h], gate).astype(BF16)


def _retention_step(zr, state, g_ret, gam_tab, s_prev, li, *, bb):
    n_b = zr.shape[0]
    st_spec = pl.BlockSpec((None, bb, RET_HEADS, HEAD_DIM, HEAD_DIM),
                           lambda i: (li, i, 0, 0, 0))
    in_arrays = [zr, state, g_ret, gam_tab]
    in_specs = [pl.BlockSpec(zr.shape, lambda i: (0, 0)), st_spec,
                _layer_spec((RET_HEADS, 1, HEAD_DIM), li, 1),
                pl.BlockSpec(gam_tab.shape, lambda i: (0, 0, 0))]
    body, extra, extra_specs, aliases = _alias_args(
        functools.partial(_retention_step_kernel, bb=bb), len(in_arrays), s_prev, 1)
    return pl.pallas_call(
        body,
        grid=(n_b // bb,),
        in_specs=in_specs + extra_specs,
        out_specs=[pl.BlockSpec((bb, RET_WIDTH), lambda i: (i, 0)), st_spec],
        out_shape=[
            jax.ShapeDtypeStruct((n_b, RET_WIDTH), BF16),
            jax.ShapeDtypeStruct(state.shape, F32),
        ],
        scratch_shapes=[pltpu.VMEM((bb, HEAD_DIM), F32)],
        input_output_aliases=aliases,
        compiler_params=_params(("arbitrary",)),
        name="retention_step",
    )(*in_arrays, *extra)


def _conv_tail(c, gcn_ref, bcn_ref, wpw_ref, bpw_ref, gate):
    mu = jnp.mean(c, axis=-1, keepdims=True)
    d = c - mu
    var = jnp.mean(d * d, axis=-1, keepdims=True)
    y = _silu(d * lax.rsqrt(var + EPS) * gcn_ref[...] + bcn_ref[...])
    y = jnp.dot(y.astype(BF16), wpw_ref[...], preferred_element_type=F32) + bpw_ref[...]
    return (y * gate).astype(BF16)


def _conv_kernel(u_ref, gc_ref, wdw_ref, bdw_ref, gcn_ref, bcn_ref, wpw_ref, bpw_ref,
                 *rest, n_cast):
    cast_in, rest = rest[:n_cast], rest[n_cast:]
    o_ref, st_ref = rest[:2]
    cast_out = rest[2:2 + n_cast]
    full_ref, c_ref, sh_ref = rest[2 + n_cast:]
    for src, dst in zip(cast_in, cast_out):
        dst[...] = src[...].astype(BF16)

    t = pl.program_id(1)
    tt, ch = u_ref.shape
    pad = HIST_ROWS - (CONV_K - 1)

    @pl.when(t == 0)
    def _():
        full_ref[0:HIST_ROWS, :] = jnp.zeros((HIST_ROWS, ch), F32)

    full_ref[HIST_ROWS:HIST_ROWS + tt, :] = u_ref[...]

    for p in range(SUBLANES):
        span = tt + SUBLANES * ((CONV_K - 1 - p) // SUBLANES)
        sh_ref[p, 0:span, :] = full_ref[pad + p:pad + p + span, :]

    for r0 in range(0, tt, CONV_ROW_BLOCK):
        for cb in range(ch // LANES):
            cs = slice(cb * LANES, (cb + 1) * LANES)
            acc = bdw_ref[:, cs]
            for j in range(CONV_K):
                p, a = j % SUBLANES, j // SUBLANES
                row = r0 + a * SUBLANES
                acc = acc + sh_ref[p, row:row + CONV_ROW_BLOCK, cs] * wdw_ref[j:j + 1, cs]
            c_ref[r0:r0 + CONV_ROW_BLOCK, cs] = acc

    o_ref[...] = _conv_tail(c_ref[...], gcn_ref, bcn_ref, wpw_ref, bpw_ref,
                            gc_ref[...].astype(F32))

    @pl.when(t == pl.num_programs(1) - 1)
    def _():
        st_ref[...] = full_ref[tt + pad:tt + HIST_ROWS, :]

    full_ref[0:HIST_ROWS, :] = full_ref[tt:tt + HIST_ROWS, :]


def _conv_weight_specs(ch, li, n_grid):
    vec = _layer_spec((1, ch), li, n_grid)
    return [_layer_spec((CONV_K, ch), li, n_grid), vec, vec, vec,
            _whole_spec((ch, ch), n_grid), vec]


def _conv(u, gc, conv_w, cast_w, cast_li, st_prev, li, depth, *, batch, seq, tt):
    m, ch = u.shape
    nt = seq // tt
    tile = pl.BlockSpec((tt, ch), lambda b, t: (b * nt + t, 0))
    casts = [_cast_specs(w, cast_li, batch * nt, lambda b, t: b * nt + t) for w in cast_w]
    in_arrays = [u, gc, *conv_w, *cast_w]
    in_specs = [tile, tile] + _conv_weight_specs(ch, li, 2) + [c[0] for c in casts]
    body, extra, extra_specs, aliases = _alias_args(
        functools.partial(_conv_kernel, n_cast=len(cast_w)), len(in_arrays), st_prev, 1)
    return pl.pallas_call(
        body,
        grid=(batch, nt),
        in_specs=in_specs + extra_specs,
        out_specs=[
            tile,
            pl.BlockSpec((None, None, CONV_K - 1, ch), lambda b, t: (li, b, 0, 0)),
        ] + [c[1] for c in casts],
        out_shape=[
            jax.ShapeDtypeStruct((m, ch), BF16),
            jax.ShapeDtypeStruct((depth, batch, CONV_K - 1, ch), F32),
        ] + [c[2] for c in casts],
        scratch_shapes=[
            pltpu.VMEM((HIST_ROWS + tt, ch), F32),
            pltpu.VMEM((tt, ch), F32),
            pltpu.VMEM((SUBLANES, tt + HIST_ROWS - SUBLANES, ch), F32),
        ],
        input_output_aliases=aliases,
        compiler_params=_params(("arbitrary", "arbitrary")),
        name="conv",
    )(*in_arrays, *extra)


def _conv_step_kernel(u_ref, gc_ref, st_ref, wdw_ref, bdw_ref, gcn_ref, bcn_ref, wpw_ref,
                      bpw_ref, o_ref, st_out_ref):
    hist = CONV_K - 1
    u = u_ref[...]
    c = u * wdw_ref[hist:hist + 1, :] + bdw_ref[...]
    for j in range(hist):
        c = c + st_ref[j] * wdw_ref[j:j + 1, :]
    o_ref[...] = _conv_tail(c, gcn_ref, bcn_ref, wpw_ref, bpw_ref,
                            gc_ref[...].astype(F32))
    for j in range(hist - 1):
        st_out_ref[j] = st_ref[j + 1]
    st_out_ref[hist - 1] = u


def _conv_step(u, gc, state, conv_w, st_prev, li, *, bb):
    n_b, ch = u.shape
    st_spec = pl.BlockSpec((None, CONV_K - 1, bb, ch), lambda i: (li, 0, i, 0))
    tile = pl.BlockSpec((bb, ch), lambda i: (i, 0))
    in_arrays = [u, gc, state, *conv_w]
    in_specs = [tile, tile, st_spec] + _conv_weight_specs(ch, li, 1)
    body, extra, extra_specs, aliases = _alias_args(
        _conv_step_kernel, len(in_arrays), st_prev, 1)
    return pl.pallas_call(
        body,
        grid=(n_b // bb,),
        in_specs=in_specs + extra_specs,
        out_specs=[tile, st_spec],
        out_shape=[
            jax.ShapeDtypeStruct((n_b, ch), BF16),
            jax.ShapeDtypeStruct(state.shape, F32),
        ],
        input_output_aliases=aliases,
        compiler_params=_params(("arbitrary",)),
        name="conv_step",
    )(*in_arrays, *extra)


def _out_tile(ret_ref, conv_ref, x_ref, p_ref, wo_ref, wg_ref, wp_ref, gpost_ref,
              gple_ref, y_ref):
    rw = ret_ref.shape[1]
    m = (jnp.dot(ret_ref[...], wo_ref[0:rw, :], preferred_element_type=F32)
         + jnp.dot(conv_ref[...], wo_ref[rw:, :], preferred_element_type=F32))
    h = x_ref[...] + _rms(m, gpost_ref[...])
    h_b = h.astype(BF16)
    e = jnp.dot(p_ref[...].astype(BF16), wp_ref[...], preferred_element_type=F32)
    e = _rms(e, gple_ref[...])
    d = h.shape[1]
    for c0 in range(0, d, GATE_CHUNK):
        cols = slice(c0, c0 + GATE_CHUNK)
        gate = jax.nn.sigmoid(jnp.dot(h_b, wg_ref[:, cols], preferred_element_type=F32))
        y_ref[:, cols] = h[:, cols] + gate * e[:, cols]


def _out_kernel(retp_ref, convp_ref, xp_ref, pp_ref, rets_ref, convs_ref, xs_ref, ps_ref,
                wo_ref, wg_ref, wp_ref, gpost_ref, gple_ref, yp_ref, ys_ref):
    i = pl.program_id(0)
    last = pl.num_programs(0) - 1
    weights = (wo_ref, wg_ref, wp_ref, gpost_ref, gple_ref)

    @pl.when(i < last)
    def _():
        _out_tile(retp_ref, convp_ref, xp_ref, pp_ref, *weights, yp_ref)

    @pl.when(i == last)
    def _():
        _out_tile(rets_ref, convs_ref, xs_ref, ps_ref, *weights, ys_ref)


def _out_proj(prompt, sample, w_out, w_gate, w_ple, g_post, g_ple, li, *, bm):
    ret_p, conv_p, xp, pp = prompt
    ret_s, conv_s, xs, ps = sample
    m, d = xp.shape
    ms = xs.shape[0]
    pd = pp.shape[2]
    n_tiles = m // bm
    tile = lambda i: jnp.minimum(i, n_tiles - 1)
    rows = lambda width: pl.BlockSpec((bm, width), lambda i: (tile(i), 0))
    whole = lambda width: _whole_spec((ms, width), 1)
    return pl.pallas_call(
        _out_kernel,
        grid=(n_tiles + 1,),
        in_specs=[
            rows(ret_p.shape[1]), rows(conv_p.shape[1]), rows(d),
            pl.BlockSpec((None, bm, pd), lambda i: (li, tile(i), 0)),
            whole(ret_s.shape[1]), whole(conv_s.shape[1]), whole(d),
            pl.BlockSpec((None, ms, pd), lambda i: (li, 0, 0)),
            _whole_spec(w_out.shape, 1, 1), _whole_spec(w_gate.shape, 1, 1),
            _whole_spec(w_ple.shape, 1, 1),
            _layer_spec((1, d), li, 1), _layer_spec((1, d), li, 1),
        ],
        out_specs=[rows(d), whole(d)],
        out_shape=[jax.ShapeDtypeStruct((m, d), F32), jax.ShapeDtypeStruct((ms, d), F32)],
        compiler_params=_params(("arbitrary",)),
        name="out_proj",
    )(ret_p, conv_p, xp, pp, ret_s, conv_s, xs, ps, w_out, w_gate, w_ple, g_post, g_ple)


def _rope_tables(pos):
    half = HEAD_DIM // 2
    inv = ROPE_BASE ** (-jnp.arange(half, dtype=F32) / half)
    ang = pos.astype(F32)[:, None] * inv[None, :]
    cos, sin = jnp.cos(ang), jnp.sin(ang)
    return jnp.concatenate([cos, cos], axis=-1), jnp.concatenate([-sin, sin], axis=-1)


def kernel(x_prompt, x_sample, state_ret, state_conv, p_prompt, p_sample, w_in, w_out,
           g_ret, w_dw, b_dw, g_cn, b_cn, w_pw2, b_pw2, g_pre, g_post, w_ple, g_ple,
           w_ple_gate):
    depth = w_in.shape[0]
    bp, lp, d = x_prompt.shape
    bs, ls, _ = x_sample.shape
    assert ls == 1 and lp % RET_CHUNK == 0

    w_in_b = w_in[0].astype(BF16)

    log_g = jnp.log1p(-jnp.exp2(-5.0 - jnp.arange(RET_HEADS, dtype=F32)))
    lg_tab = jnp.broadcast_to(log_g[:, None, None], (RET_HEADS, 1, RET_CHUNK))
    gam_tab = jnp.broadcast_to(jnp.exp(log_g)[:, None, None], (RET_HEADS, 1, HEAD_DIM))
    rope_p = _rope_tables(jnp.arange(lp, dtype=jnp.int32))
    rope_s = _rope_tables(jnp.full((bs,), PAST_LEN, dtype=jnp.int32))

    vec = lambda a: a.reshape(depth, 1, -1)
    g_pre3, g_post3, g_ple3 = vec(g_pre), vec(g_post), vec(g_ple)
    g_ret4 = g_ret.reshape(depth, RET_HEADS, 1, HEAD_DIM)
    conv_vecs = (vec(b_dw), vec(g_cn), vec(b_cn))
    b_pw3 = vec(b_pw2)
    state_conv_t = jnp.transpose(state_conv, (0, 2, 1, 3))
    pp = p_prompt.reshape(depth, bp * lp, -1)
    ps = p_sample.reshape(depth, bs, -1)

    bm_in, bm_out, tt = 256, 512, 256
    xp = x_prompt.reshape(bp * lp, d)
    xs = x_sample.reshape(bs, d)
    ret_p = conv_p = ret_s = conv_s = None
    for li in range(depth):
        (zr_p, u_p, gc_p, zr_s, u_s, gc_s) = _in_proj(
            xp, xs, g_pre3, w_in_b, rope_p, rope_s, li, bm=bm_in, pos_blocks=lp // bm_in)

        if li == 0:
            r_p, ret_p, w_ple_b, w_out_b, w_gate_b, w_pw_b = _retention(
                zr_p, g_ret4, lg_tab, (w_ple, w_out, w_ple_gate, w_pw2), ret_p, li, depth,
                batch=bp, seq=lp)
        else:
            r_p, ret_p, w_ple_b = _retention(zr_p, g_ret4, lg_tab, (w_ple,), ret_p, li,
                                             depth, batch=bp, seq=lp)
        conv_w = (w_dw, *conv_vecs, w_pw_b, b_pw3)
        out_w = (w_out_b, w_gate_b, w_ple_b)
        next_w = (w_in, w_out, w_ple_gate, w_pw2) if li + 1 < depth else ()
        c_p, conv_p, *next_b = _conv(u_p, gc_p, conv_w, next_w, li + 1, conv_p, li, depth,
                                     batch=bp, seq=lp, tt=tt)
        if next_b:
            w_in_b, w_out_b, w_gate_b, w_pw_b = next_b

        r_s, ret_s = _retention_step(zr_s, state_ret, g_ret4, gam_tab, ret_s, li, bb=16)
        c_s, conv_s = _conv_step(u_s, gc_s, state_conv_t, conv_w, conv_s, li, bb=16)

        xp, xs = _out_proj((r_p, c_p, xp, pp), (r_s, c_s, xs, ps),
                           *out_w, g_post3, g_ple3, li, bm=bm_out)

    return (xp.reshape(bp, lp, d), xs.reshape(bs, ls, d), ret_p, conv_p, ret_s,
            jnp.transpose(conv_s, (0, 2, 1, 3)))
```

```python
import functools

import jax
import jax.numpy as jnp
from jax import lax
from jax.experimental import pallas as pl
from jax.experimental.pallas import tpu as pltpu

F32 = jnp.float32
BF16 = jnp.bfloat16

RET_HEADS = 8
HEAD_DIM = 128
RET_WIDTH = RET_HEADS * HEAD_DIM
CONV_K = 31
ROPE_BASE = 10000.0
EPS = 1e-6
PAST_LEN = 16384

VMEM_LIMIT_BYTES = 58 * 1024 * 1024
LANES = 128
SUBLANES = 8
COL_TILE = 1024
RET_CHUNK = 256
CONV_ROW_BLOCK = 64
HIST_ROWS = 32
GATE_CHUNK = 512


def _params(sem):
    return pltpu.CompilerParams(dimension_semantics=sem,
                                vmem_limit_bytes=VMEM_LIMIT_BYTES)


def _silu(x):
    return x * jax.nn.sigmoid(x)


def _rms(x, g):
    return x * lax.rsqrt(jnp.mean(x * x, axis=-1, keepdims=True) + EPS) * g


def _layer_spec(shape, li, n_grid, buffered=None):
    zeros = (0,) * len(shape)
    if n_grid == 1:
        index_map = lambda i: (li,) + zeros
    else:
        index_map = lambda i, j: (li,) + zeros
    if buffered is None:
        return pl.BlockSpec((None,) + tuple(shape), index_map)
    return pl.BlockSpec((None,) + tuple(shape), index_map,
                        pipeline_mode=pl.Buffered(buffered))


def _whole_spec(shape, n_grid, buffered=None):
    zeros = (0,) * len(shape)
    index_map = (lambda i: zeros) if n_grid == 1 else (lambda i, j: zeros)
    if buffered is None:
        return pl.BlockSpec(tuple(shape), index_map)
    return pl.BlockSpec(tuple(shape), index_map, pipeline_mode=pl.Buffered(buffered))


def _cast_specs(w, li, n_steps, step_of):
    _, rows, cols = w.shape
    r = rows // n_steps
    assert r * n_steps == rows and r % 16 == 0
    return (pl.BlockSpec((None, r, cols), lambda *g: (li, step_of(*g), 0)),
            pl.BlockSpec((r, cols), lambda *g: (step_of(*g), 0)),
            jax.ShapeDtypeStruct((rows, cols), BF16))


def _with_alias_slot(body, n_in):
    def wrapped(*refs):
        return body(*refs[:n_in], *refs[n_in + 1:])
    return wrapped


def _alias_args(body, n_in, prev, out_idx):
    if prev is None:
        return body, [], [], {}
    return (_with_alias_slot(body, n_in), [prev],
            [pl.BlockSpec(memory_space=pl.ANY)], {n_in: out_idx})


def _in_proj_tile(x_ref, gpre_ref, w_ref, cos_ref, sin_ref, zr_ref, u_ref, gc_ref,
                  wait_group=None):
    hn = _rms(x_ref[...], gpre_ref[...]).astype(BF16)

    def proj(g):
        if wait_group is not None:
            wait_group(g)
        return jnp.dot(hn, w_ref[:, g * COL_TILE:(g + 1) * COL_TILE],
                       preferred_element_type=F32)

    def rope(z, scale, col0):
        c = cos_ref[...]
        s = sin_ref[...]
        for h in range(RET_HEADS):
            xh = z[:, h * HEAD_DIM:(h + 1) * HEAD_DIM]
            rot = pltpu.roll(xh, HEAD_DIM // 2, axis=1)
            r = xh * c + rot * s
            if scale is not None:
                r = r * scale
            zr_ref[:, col0 + h * HEAD_DIM:col0 + (h + 1) * HEAD_DIM] = r.astype(BF16)

    rope(proj(0), None, 0)
    rope(proj(1), HEAD_DIM ** -0.5, COL_TILE)
    zr_ref[:, 3 * COL_TILE:4 * COL_TILE] = _silu(proj(3)).astype(BF16)
    u_ref[...] = proj(4) * jax.nn.sigmoid(proj(5))
    gc_ref[...] = _silu(proj(6)).astype(BF16)
    zr_ref[:, 2 * COL_TILE:3 * COL_TILE] = proj(2).astype(BF16)


def _in_proj_kernel(xp_ref, xs_ref, gpre_ref, w_hbm_ref, cosp_ref, sinp_ref, coss_ref,
                    sins_ref, zrp_ref, up_ref, gcp_ref, zrs_ref, us_ref, gcs_ref,
                    w_ref, w_sem):
    i = pl.program_id(0)
    last = pl.num_programs(0) - 1
    n_groups = w_ref.shape[1] // COL_TILE
    prompt = (xp_ref, gpre_ref, w_ref, cosp_ref, sinp_ref, zrp_ref, up_ref, gcp_ref)

    def group_copy(g):
        cols = pl.ds(g * COL_TILE, COL_TILE)
        return pltpu.make_async_copy(w_hbm_ref.at[:, cols], w_ref.at[:, cols], w_sem.at[g])

    @pl.when(i == 0)
    def _():
        for g in range(n_groups):
            group_copy(g).start()
        _in_proj_tile(*prompt, wait_group=lambda g: group_copy(g).wait())

    @pl.when((i > 0) & (i < last))
    def _():
        _in_proj_tile(*prompt)

    @pl.when(i == last)
    def _():
        _in_proj_tile(xs_ref, gpre_ref, w_ref, coss_ref, sins_ref, zrs_ref, us_ref, gcs_ref)


def _group_out_shapes(m):
    return [
        jax.ShapeDtypeStruct((m, 4 * COL_TILE), BF16),
        jax.ShapeDtypeStruct((m, COL_TILE), F32),
        jax.ShapeDtypeStruct((m, COL_TILE), BF16),
    ]


def _in_proj(xp, xs, g_pre, w_in, rope_p, rope_s, li, *, bm, pos_blocks):
    m, d = xp.shape
    ms = xs.shape[0]
    n_cols = w_in.shape[1]
    n_tiles = m // bm
    assert n_cols == 7 * COL_TILE and m % bm == 0
    tile = lambda i: jnp.minimum(i, n_tiles - 1)
    rows = lambda width: pl.BlockSpec((bm, width), lambda i: (tile(i), 0))
    pos = pl.BlockSpec((bm, HEAD_DIM), lambda i: (tile(i) % pos_blocks, 0))
    whole = lambda width: _whole_spec((ms, width), 1)
    return pl.pallas_call(
        _in_proj_kernel,
        grid=(n_tiles + 1,),
        in_specs=[rows(d), whole(d), _layer_spec((1, d), li, 1),
                  pl.BlockSpec(memory_space=pl.ANY), pos, pos,
                  whole(HEAD_DIM), whole(HEAD_DIM)],
        out_specs=[rows(4 * COL_TILE), rows(COL_TILE), rows(COL_TILE),
                   whole(4 * COL_TILE), whole(COL_TILE), whole(COL_TILE)],
        out_shape=_group_out_shapes(m) + _group_out_shapes(ms),
        scratch_shapes=[pltpu.VMEM((d, n_cols), BF16),
                        pltpu.SemaphoreType.DMA((n_cols // COL_TILE,))],
        compiler_params=_params(("arbitrary",)),
        name="in_proj",
    )(xp, xs, g_pre, w_in, *rope_p, *rope_s)


def _group_norm_gate(o, g_ret, gate):
    mu = jnp.mean(o, axis=-1, keepdims=True)
    d = o - mu
    var = jnp.mean(d * d, axis=-1, keepdims=True)
    return (d * lax.rsqrt(var + EPS)) * g_ret * gate


def _retention_kernel(q_ref, k_ref, v_ref, gr_ref, gret_ref, lg_ref, *rest, n_cast):
    cast_in, rest = rest[:n_cast], rest[n_cast:]
    o_ref, s_out_ref = rest[:2]
    cast_out = rest[2:2 + n_cast]
    s_ref, decay_ref, cross_ref, kdec_ref = rest[2 + n_cast:]
    for src, dst in zip(cast_in, cast_out):
        dst[...] = src[...].astype(BF16)

    c = RET_CHUNK
    b = pl.program_id(0)
    t = pl.program_id(1)

    @pl.when((b == 0) & (t == 0))
    def _():
        row = lax.broadcasted_iota(jnp.int32, (c, c), 0)
        col = lax.broadcasted_iota(jnp.int32, (c, c), 1)
        rel = (row - col).astype(F32)
        idx = lax.broadcasted_iota(jnp.int32, (c, LANES), 0).astype(F32)
        for h in range(RET_HEADS):
            lg = lg_ref[h]
            decay_ref[h] = jnp.where(rel >= 0, jnp.exp(lg * jnp.maximum(rel, 0.0)), 0.0)
            cross_ref[h] = jnp.exp(lg[:, :LANES] * (idx + 1.0))
            kdec_ref[h] = jnp.exp(lg[:, :LANES] * (c - 1.0 - idx))

    @pl.when(t == 0)
    def _():
        s_ref[...] = jnp.zeros(s_ref.shape, F32)

    for ci in range(q_ref.shape[0] // c):
        rows = slice(ci * c, (ci + 1) * c)
        for h in range(RET_HEADS):
            cols = slice(h * HEAD_DIM, (h + 1) * HEAD_DIM)
            q = q_ref[rows, cols]
            k = k_ref[rows, cols]
            v = v_ref[rows, cols]
            s = s_ref[h]
            g_chunk = jnp.exp(lg_ref[h][:, :LANES] * float(c))
            scores = lax.dot_general(q, k, (((1,), (1,)), ((), ())),
                                     preferred_element_type=F32) * decay_ref[h]
            o = jnp.dot(scores.astype(BF16), v, preferred_element_type=F32)
            o = o + jnp.dot(q, s.astype(BF16), preferred_element_type=F32) * cross_ref[h]
            kd = (k.astype(F32) * kdec_ref[h]).astype(BF16)
            s_ref[h] = g_chunk * s + lax.dot_general(
                kd, v, (((0,), (0,)), ((), ())), preferred_element_type=F32)
            o_ref[rows, cols] = _group_norm_gate(
                o, gret_ref[h], gr_ref[rows, cols].astype(F32)).astype(BF16)

    @pl.when(t == pl.num_programs(1) - 1)
    def _():
        s_out_ref[...] = s_ref[...]


def _retention(zr, g_ret, lg_tab, cast_w, s_prev, li, depth, *, batch, seq, tq):
    m = zr.shape[0]
    nt = seq // tq
    blk = lambda g: pl.BlockSpec((tq, RET_WIDTH), lambda b, t: (b * nt + t, g))
    casts = [_cast_specs(w, li, batch * nt, lambda b, t: b * nt + t) for w in cast_w]
    in_arrays = [zr, zr, zr, zr, g_ret, lg_tab, *cast_w]
    in_specs = [blk(0), blk(1), blk(2), blk(3),
                _layer_spec((RET_HEADS, 1, HEAD_DIM), li, 2),
                pl.BlockSpec(lg_tab.shape, lambda b, t: (0, 0, 0))] + [c[0] for c in casts]
    body, extra, extra_specs, aliases = _alias_args(
        functools.partial(_retention_kernel, n_cast=len(cast_w)),
        len(in_arrays), s_prev, 1)
    return pl.pallas_call(
        body,
        grid=(batch, nt),
        in_specs=in_specs + extra_specs,
        out_specs=[
            pl.BlockSpec((tq, RET_WIDTH), lambda b, t: (b * nt + t, 0)),
            pl.BlockSpec((None, None, RET_HEADS, HEAD_DIM, HEAD_DIM),
                         lambda b, t: (li, b, 0, 0, 0)),
        ] + [c[1] for c in casts],
        out_shape=[
            jax.ShapeDtypeStruct((m, RET_WIDTH), BF16),
            jax.ShapeDtypeStruct((depth, batch, RET_HEADS, HEAD_DIM, HEAD_DIM), F32),
        ] + [c[2] for c in casts],
        scratch_shapes=[
            pltpu.VMEM((RET_HEADS, HEAD_DIM, HEAD_DIM), F32),
            pltpu.VMEM((RET_HEADS, RET_CHUNK, RET_CHUNK), F32),
            pltpu.VMEM((RET_HEADS, RET_CHUNK, LANES), F32),
            pltpu.VMEM((RET_HEADS, RET_CHUNK, LANES), F32),
        ],
        input_output_aliases=aliases,
        compiler_params=_params(("arbitrary", "arbitrary")),
        name="retention",
    )(*in_arrays, *extra)


def _retention_step_kernel(zr_ref, s_ref, gret_ref, gam_ref, o_ref, s_out_ref, oc_ref,
                           *, bb):
    i = pl.program_id(0)
    n_b = zr_ref.shape[0]
    rows = pl.ds(pl.multiple_of(i * bb, bb), bb)
    shift = (n_b - i * bb) % n_b
    for h in range(RET_HEADS):
        hs = lambda g: slice((g * RET_HEADS + h) * HEAD_DIM,
                             (g * RET_HEADS + h + 1) * HEAD_DIM)
        q_t = pltpu.roll(zr_ref[:, hs(0)].astype(F32).T, shift, axis=1)
        k_t = pltpu.roll(zr_ref[:, hs(1)].astype(F32).T, shift, axis=1)
        q = zr_ref[rows, hs(0)].astype(F32)
        k = zr_ref[rows, hs(1)].astype(F32)
        v = zr_ref[rows, hs(2)].astype(F32)
        gate = zr_ref[rows, hs(3)].astype(F32)
        gam = gam_ref[h]
        for b in range(bb):
            s = s_ref[b, h]
            oc_ref[b:b + 1, :] = jnp.sum(q_t[:, b:b + 1] * s, axis=0, keepdims=True)
            s_out_ref[b, h] = gam * s + k_t[:, b:b + 1] * v[b:b + 1, :]
        o = jnp.sum(q * k, axis=-1, keepdims=True) * v + oc_ref[...] * gam
        o_ref[:, hs(0)] = _group_norm_gate(o, gret_ref[h], gate).astype(BF16)


def _retention_step(zr, state, g_ret, gam_tab, s_prev, li, *, bb):
    n_b = zr.shape[0]
    st_spec = pl.BlockSpec((None, bb, RET_HEADS, HEAD_DIM, HEAD_DIM),
                           lambda i: (li, i, 0, 0, 0))
    in_arrays = [zr, state, g_ret, gam_tab]
    in_specs = [pl.BlockSpec(zr.shape, lambda i: (0, 0)), st_spec,
                _layer_spec((RET_HEADS, 1, HEAD_DIM), li, 1),
                pl.BlockSpec(gam_tab.shape, lambda i: (0, 0, 0))]
    body, extra, extra_specs, aliases = _alias_args(
        functools.partial(_retention_step_kernel, bb=bb), len(in_arrays), s_prev, 1)
    return pl.pallas_call(
        body,
        grid=(n_b // bb,),
        in_specs=in_specs + extra_specs,
        out_specs=[pl.BlockSpec((bb, RET_WIDTH), lambda i: (i, 0)), st_spec],
        out_shape=[
            jax.ShapeDtypeStruct((n_b, RET_WIDTH), BF16),
            jax.ShapeDtypeStruct(state.shape, F32),
        ],
        scratch_shapes=[pltpu.VMEM((bb, HEAD_DIM), F32)],
        input_output_aliases=aliases,
        compiler_params=_params(("arbitrary",)),
        name="retention_step",
    )(*in_arrays, *extra)


def _conv_tail(c, gcn_ref, bcn_ref, wpw_ref, bpw_ref, gate):
    mu = jnp.mean(c, axis=-1, keepdims=True)
    d = c - mu
    var = jnp.mean(d * d, axis=-1, keepdims=True)
    y = _silu(d * lax.rsqrt(var + EPS) * gcn_ref[...] + bcn_ref[...])
    y = jnp.dot(y.astype(BF16), wpw_ref[...], preferred_element_type=F32) + bpw_ref[...]
    return (y * gate).astype(BF16)


def _conv_kernel(u_ref, gc_ref, wdw_ref, bdw_ref, gcn_ref, bcn_ref, wpw_ref, bpw_ref,
                 *rest, n_cast):
    cast_in, rest = rest[:n_cast], rest[n_cast:]
    o_ref, st_ref = rest[:2]
    cast_out = rest[2:2 + n_cast]
    full_ref, c_ref, sh_ref = rest[2 + n_cast:]
    for src, dst in zip(cast_in, cast_out):
        dst[...] = src[...].astype(BF16)

    t = pl.program_id(1)
    tt, ch = u_ref.shape
    pad = HIST_ROWS - (CONV_K - 1)

    @pl.when(t == 0)
    def _():
        full_ref[0:HIST_ROWS, :] = jnp.zeros((HIST_ROWS, ch), F32)

    full_ref[HIST_ROWS:HIST_ROWS + tt, :] = u_ref[...]

    for p in range(SUBLANES):
        span = tt + SUBLANES * ((CONV_K - 1 - p) // SUBLANES)
        sh_ref[p, 0:span, :] = full_ref[pad + p:pad + p + span, :]

    for r0 in range(0, tt, CONV_ROW_BLOCK):
        for cb in range(ch // LANES):
            cs = slice(cb * LANES, (cb + 1) * LANES)
            acc = bdw_ref[:, cs]
            for j in range(CONV_K):
                p, a = j % SUBLANES, j // SUBLANES
                row = r0 + a * SUBLANES
                acc = acc + sh_ref[p, row:row + CONV_ROW_BLOCK, cs] * wdw_ref[j:j + 1, cs]
            c_ref[r0:r0 + CONV_ROW_BLOCK, cs] = acc

    o_ref[...] = _conv_tail(c_ref[...], gcn_ref, bcn_ref, wpw_ref, bpw_ref,
                            gc_ref[...].astype(F32))

    @pl.when(t == pl.num_programs(1) - 1)
    def _():
        st_ref[...] = full_ref[tt + pad:tt + HIST_ROWS, :]

    full_ref[0:HIST_ROWS, :] = full_ref[tt:tt + HIST_ROWS, :]


def _conv_weight_specs(ch, li, n_grid):
    vec = _layer_spec((1, ch), li, n_grid)
    return [_layer_spec((CONV_K, ch), li, n_grid), vec, vec, vec,
            _whole_spec((ch, ch), n_grid), vec]


def _conv(u, gc, conv_w, cast_w, cast_li, st_prev, li, depth, *, batch, seq, tt):
    m, ch = u.shape
    nt = seq // tt
    tile = pl.BlockSpec((tt, ch), lambda b, t: (b * nt + t, 0))
    casts = [_cast_specs(w, cast_li, batch * nt, lambda b, t: b * nt + t) for w in cast_w]
    in_arrays = [u, gc, *conv_w, *cast_w]
    in_specs = [tile, tile] + _conv_weight_specs(ch, li, 2) + [c[0] for c in casts]
    body, extra, extra_specs, aliases = _alias_args(
        functools.partial(_conv_kernel, n_cast=len(cast_w)), len(in_arrays), st_prev, 1)
    return pl.pallas_call(
        body,
        grid=(batch, nt),
        in_specs=in_specs + extra_specs,
        out_specs=[
            tile,
            pl.BlockSpec((None, None, CONV_K - 1, ch), lambda b, t: (li, b, 0, 0)),
        ] + [c[1] for c in casts],
        out_shape=[
            jax.ShapeDtypeStruct((m, ch), BF16),
            jax.ShapeDtypeStruct((depth, batch, CONV_K - 1, ch), F32),
        ] + [c[2] for c in casts],
        scratch_shapes=[
            pltpu.VMEM((HIST_ROWS + tt, ch), F32),
            pltpu.VMEM((tt, ch), F32),
            pltpu.VMEM((SUBLANES, tt + HIST_ROWS - SUBLANES, ch), F32),
        ],
        input_output_aliases=aliases,
        compiler_params=_params(("arbitrary", "arbitrary")),
        name="conv",
    )(*in_arrays, *extra)


def _conv_step_kernel(u_ref, gc_ref, st_ref, wdw_ref, bdw_ref, gcn_ref, bcn_ref, wpw_ref,
                      bpw_ref, o_ref, st_out_ref):
    hist = CONV_K - 1
    u = u_ref[...]
    c = u * wdw_ref[hist:hist + 1, :] + bdw_ref[...]
    for j in range(hist):
        c = c + st_ref[j] * wdw_ref[j:j + 1, :]
    o_ref[...] = _conv_tail(c, gcn_ref, bcn_ref, wpw_ref, bpw_ref,
                            gc_ref[...].astype(F32))
    for j in range(hist - 1):
        st_out_ref[j] = st_ref[j + 1]
    st_out_ref[hist - 1] = u


def _conv_step(u, gc, state, conv_w, st_prev, li, *, bb):
    n_b, ch = u.shape
    st_spec = pl.BlockSpec((None, CONV_K - 1, bb, ch), lambda i: (li, 0, i, 0))
    tile = pl.BlockSpec((bb, ch), lambda i: (i, 0))
    in_arrays = [u, gc, state, *conv_w]
    in_specs = [tile, tile, st_spec] + _conv_weight_specs(ch, li, 1)
    body, extra, extra_specs, aliases = _alias_args(
        _conv_step_kernel, len(in_arrays), st_prev, 1)
    return pl.pallas_call(
        body,
        grid=(n_b // bb,),
        in_specs=in_specs + extra_specs,
        out_specs=[tile, st_spec],
        out_shape=[
            jax.ShapeDtypeStruct((n_b, ch), BF16),
            jax.ShapeDtypeStruct(state.shape, F32),
        ],
        input_output_aliases=aliases,
        compiler_params=_params(("arbitrary",)),
        name="conv_step",
    )(*in_arrays, *extra)


def _out_tile(ret_ref, conv_ref, x_ref, p_ref, wo_ref, wg_ref, wp_ref, gpost_ref,
              gple_ref, y_ref, wait_weight=None):
    wait = wait_weight if wait_weight is not None else (lambda k: None)
    rw = ret_ref.shape[1]
    wait(0)
    m = (jnp.dot(ret_ref[...], wo_ref[0:rw, :], preferred_element_type=F32)
         + jnp.dot(conv_ref[...], wo_ref[rw:, :], preferred_element_type=F32))
    h = x_ref[...] + _rms(m, gpost_ref[...])
    h_b = h.astype(BF16)
    wait(2)
    e = jnp.dot(p_ref[...].astype(BF16), wp_ref[...], preferred_element_type=F32)
    e = _rms(e, gple_ref[...])
    wait(1)
    d = h.shape[1]
    for c0 in range(0, d, GATE_CHUNK):
        cols = slice(c0, c0 + GATE_CHUNK)
        gate = jax.nn.sigmoid(jnp.dot(h_b, wg_ref[:, cols], preferred_element_type=F32))
        y_ref[:, cols] = h[:, cols] + gate * e[:, cols]


def _out_kernel(retp_ref, convp_ref, xp_ref, pp_ref, rets_ref, convs_ref, xs_ref, ps_ref,
                wo_hbm_ref, wg_hbm_ref, wp_hbm_ref, gpost_ref, gple_ref, yp_ref, ys_ref,
                wo_ref, wg_ref, wp_ref, w_sem):
    i = pl.program_id(0)
    last = pl.num_programs(0) - 1
    weights = (wo_ref, wg_ref, wp_ref, gpost_ref, gple_ref)

    def weight_copy(k):
        src, dst = ((wo_hbm_ref, wo_ref), (wg_hbm_ref, wg_ref), (wp_hbm_ref, wp_ref))[k]
        return pltpu.make_async_copy(src, dst, w_sem.at[k])

    @pl.when(i == 0)
    def _():
        for k in (0, 2, 1):
            weight_copy(k).start()
        _out_tile(retp_ref, convp_ref, xp_ref, pp_ref, *weights, yp_ref,
                  wait_weight=lambda k: weight_copy(k).wait())

    @pl.when((i > 0) & (i < last))
    def _():
        _out_tile(retp_ref, convp_ref, xp_ref, pp_ref, *weights, yp_ref)

    @pl.when(i == last)
    def _():
        _out_tile(rets_ref, convs_ref, xs_ref, ps_ref, *weights, ys_ref)


def _out_proj(prompt, sample, w_out, w_gate, w_ple, g_post, g_ple, li, *, bm):
    ret_p, conv_p, xp, pp = prompt
    ret_s, conv_s, xs, ps = sample
    m, d = xp.shape
    ms = xs.shape[0]
    pd = pp.shape[2]
    n_tiles = m // bm
    tile = lambda i: jnp.minimum(i, n_tiles - 1)
    rows = lambda width: pl.BlockSpec((bm, width), lambda i: (tile(i), 0))
    whole = lambda width: _whole_spec((ms, width), 1)
    return pl.pallas_call(
        _out_kernel,
        grid=(n_tiles + 1,),
        in_specs=[
            rows(ret_p.shape[1]), rows(conv_p.shape[1]), rows(d),
            pl.BlockSpec((None, bm, pd), lambda i: (li, tile(i), 0)),
            whole(ret_s.shape[1]), whole(conv_s.shape[1]), whole(d),
            pl.BlockSpec((None, ms, pd), lambda i: (li, 0, 0)),
            pl.BlockSpec(memory_space=pl.ANY), pl.BlockSpec(memory_space=pl.ANY),
            pl.BlockSpec(memory_space=pl.ANY),
            _layer_spec((1, d), li, 1), _layer_spec((1, d), li, 1),
        ],
        out_specs=[rows(d), whole(d)],
        out_shape=[jax.ShapeDtypeStruct((m, d), F32), jax.ShapeDtypeStruct((ms, d), F32)],
        scratch_shapes=[pltpu.VMEM(w_out.shape, BF16), pltpu.VMEM(w_gate.shape, BF16),
                        pltpu.VMEM(w_ple.shape, BF16), pltpu.SemaphoreType.DMA((3,))],
        compiler_params=_params(("arbitrary",)),
        name="out_proj",
    )(ret_p, conv_p, xp, pp, ret_s, conv_s, xs, ps, w_out, w_gate, w_ple, g_post, g_ple)


def _rope_tables(pos):
    half = HEAD_DIM // 2
    inv = ROPE_BASE ** (-jnp.arange(half, dtype=F32) / half)
    ang = pos.astype(F32)[:, None] * inv[None, :]
    cos, sin = jnp.cos(ang), jnp.sin(ang)
    return jnp.concatenate([cos, cos], axis=-1), jnp.concatenate([-sin, sin], axis=-1)


def kernel(x_prompt, x_sample, state_ret, state_conv, p_prompt, p_sample, w_in, w_out,
           g_ret, w_dw, b_dw, g_cn, b_cn, w_pw2, b_pw2, g_pre, g_post, w_ple, g_ple,
           w_ple_gate):
    depth = w_in.shape[0]
    bp, lp, d = x_prompt.shape
    bs, ls, _ = x_sample.shape
    assert ls == 1 and lp % RET_CHUNK == 0

    w_in_b = w_in[0].astype(BF16)

    log_g = jnp.log1p(-jnp.exp2(-5.0 - jnp.arange(RET_HEADS, dtype=F32)))
    lg_tab = jnp.broadcast_to(log_g[:, None, None], (RET_HEADS, 1, RET_CHUNK))
    gam_tab = jnp.broadcast_to(jnp.exp(log_g)[:, None, None], (RET_HEADS, 1, HEAD_DIM))
    rope_p = _rope_tables(jnp.arange(lp, dtype=jnp.int32))
    rope_s = _rope_tables(jnp.full((bs,), PAST_LEN, dtype=jnp.int32))

    vec = lambda a: a.reshape(depth, 1, -1)
    g_pre3, g_post3, g_ple3 = vec(g_pre), vec(g_post), vec(g_ple)
    g_ret4 = g_ret.reshape(depth, RET_HEADS, 1, HEAD_DIM)
    conv_vecs = (vec(b_dw), vec(g_cn), vec(b_cn))
    b_pw3 = vec(b_pw2)
    state_conv_t = jnp.transpose(state_conv, (0, 2, 1, 3))
    pp = p_prompt.reshape(depth, bp * lp, -1)
    ps = p_sample.reshape(depth, bs, -1)

    bm_in, bm_out, tq, tt = 256, 512, 512, 256
    xp = x_prompt.reshape(bp * lp, d)
    xs = x_sample.reshape(bs, d)
    ret_p = conv_p = ret_s = conv_s = None
    for li in range(depth):
        (zr_p, u_p, gc_p, zr_s, u_s, gc_s) = _in_proj(
            xp, xs, g_pre3, w_in_b, rope_p, rope_s, li, bm=bm_in, pos_blocks=lp // bm_in)

        if li == 0:
            r_p, ret_p, w_ple_b, w_out_b, w_gate_b, w_pw_b = _retention(
                zr_p, g_ret4, lg_tab, (w_ple, w_out, w_ple_gate, w_pw2), ret_p, li, depth,
                batch=bp, seq=lp, tq=tq)
        else:
            r_p, ret_p, w_ple_b = _retention(zr_p, g_ret4, lg_tab, (w_ple,), ret_p, li,
                                             depth, batch=bp, seq=lp, tq=tq)
        conv_w = (w_dw, *conv_vecs, w_pw_b, b_pw3)
        out_w = (w_out_b, w_gate_b, w_ple_b)
        next_w = (w_in, w_out, w_ple_gate, w_pw2) if li + 1 < depth else ()
        c_p, conv_p, *next_b = _conv(u_p, gc_p, conv_w, next_w, li + 1, conv_p, li, depth,
                                     batch=bp, seq=lp, tt=tt)
        if next_b:
            w_in_b, w_out_b, w_gate_b, w_pw_b = next_b

        r_s, ret_s = _retention_step(zr_s, state_ret, g_ret4, gam_tab, ret_s, li, bb=16)
        c_s, conv_s = _conv_step(u_s, gc_s, state_conv_t, conv_w, conv_s, li, bb=16)

        xp, xs = _out_proj((r_p, c_p, xp, pp), (r_s, c_s, xs, ps),
                           *out_w, g_post3, g_ple3, li, bm=bm_out)

    return (xp.reshape(bp, lp, d), xs.reshape(bs, ls, d), ret_p, conv_p, ret_s,
            jnp.transpose(conv_s, (0, 2, 1, 3)))
```

```python
import functools

import jax
import jax.numpy as jnp
from jax import lax
from jax.experimental import pallas as pl
from jax.experimental.pallas import tpu as pltpu

F32 = jnp.float32
BF16 = jnp.bfloat16

RET_HEADS = 8
HEAD_DIM = 128
RET_WIDTH = RET_HEADS * HEAD_DIM
CONV_K = 31
ROPE_BASE = 10000.0
EPS = 1e-6
PAST_LEN = 16384

VMEM_LIMIT_BYTES = 58 * 1024 * 1024
LANES = 128
SUBLANES = 8
COL_TILE = 1024
RET_CHUNK = 256
CONV_ROW_BLOCK = 64
HIST_ROWS = 32
GATE_CHUNK = 512


def _params(sem):
    return pltpu.CompilerParams(dimension_semantics=sem,
                                vmem_limit_bytes=VMEM_LIMIT_BYTES)


def _silu(x):
    return x * jax.nn.sigmoid(x)


def _rms(x, g):
    return x * lax.rsqrt(jnp.mean(x * x, axis=-1, keepdims=True) + EPS) * g


def _layer_spec(shape, li, n_grid, buffered=None):
    zeros = (0,) * len(shape)
    if n_grid == 1:
        index_map = lambda i: (li,) + zeros
    else:
        index_map = lambda i, j: (li,) + zeros
    if buffered is None:
        return pl.BlockSpec((None,) + tuple(shape), index_map)
    return pl.BlockSpec((None,) + tuple(shape), index_map,
                        pipeline_mode=pl.Buffered(buffered))


def _whole_spec(shape, n_grid, buffered=None):
    zeros = (0,) * len(shape)
    index_map = (lambda i: zeros) if n_grid == 1 else (lambda i, j: zeros)
    if buffered is None:
        return pl.BlockSpec(tuple(shape), index_map)
    return pl.BlockSpec(tuple(shape), index_map, pipeline_mode=pl.Buffered(buffered))


def _cast_specs(w, li, n_steps, step_of):
    _, rows, cols = w.shape
    r = rows // n_steps
    assert r * n_steps == rows and r % 16 == 0
    return (pl.BlockSpec((None, r, cols), lambda *g: (li, step_of(*g), 0)),
            pl.BlockSpec((r, cols), lambda *g: (step_of(*g), 0)),
            jax.ShapeDtypeStruct((rows, cols), BF16))


def _with_alias_slot(body, n_in):
    def wrapped(*refs):
        return body(*refs[:n_in], *refs[n_in + 1:])
    return wrapped


def _alias_args(body, n_in, prev, out_idx):
    if prev is None:
        return body, [], [], {}
    return (_with_alias_slot(body, n_in), [prev],
            [pl.BlockSpec(memory_space=pl.ANY)], {n_in: out_idx})


def _in_proj_tile(x_ref, gpre_ref, w_ref, cos_ref, sin_ref, zr_ref, u_ref, gc_ref,
                  wait_group=None):
    hn = _rms(x_ref[...], gpre_ref[...]).astype(BF16)

    def proj(g):
        if wait_group is not None:
            wait_group(g)
        return jnp.dot(hn, w_ref[:, g * COL_TILE:(g + 1) * COL_TILE],
                       preferred_element_type=F32)

    def rope(z, scale, col0):
        c = cos_ref[...]
        s = sin_ref[...]
        for h in range(RET_HEADS):
            xh = z[:, h * HEAD_DIM:(h + 1) * HEAD_DIM]
            rot = pltpu.roll(xh, HEAD_DIM // 2, axis=1)
            r = xh * c + rot * s
            if scale is not None:
                r = r * scale
            zr_ref[:, col0 + h * HEAD_DIM:col0 + (h + 1) * HEAD_DIM] = r.astype(BF16)

    rope(proj(0), None, 0)
    rope(proj(1), HEAD_DIM ** -0.5, COL_TILE)
    zr_ref[:, 3 * COL_TILE:4 * COL_TILE] = _silu(proj(3)).astype(BF16)
    u_ref[...] = proj(4) * jax.nn.sigmoid(proj(5))
    gc_ref[...] = _silu(proj(6)).astype(BF16)
    zr_ref[:, 2 * COL_TILE:3 * COL_TILE] = proj(2).astype(BF16)


def _in_proj_kernel(xp_ref, xs_ref, gpre_ref, w_hbm_ref, cosp_ref, sinp_ref, coss_ref,
                    sins_ref, zrp_ref, up_ref, gcp_ref, zrs_ref, us_ref, gcs_ref,
                    w_ref, w_sem):
    i = pl.program_id(0)
    last = pl.num_programs(0) - 1
    n_groups = w_ref.shape[1] // COL_TILE
    prompt = (xp_ref, gpre_ref, w_ref, cosp_ref, sinp_ref, zrp_ref, up_ref, gcp_ref)

    def group_copy(g):
        cols = pl.ds(g * COL_TILE, COL_TILE)
        return pltpu.make_async_copy(w_hbm_ref.at[:, cols], w_ref.at[:, cols], w_sem.at[g])

    @pl.when(i == 0)
    def _():
        for g in range(n_groups):
            group_copy(g).start()
        _in_proj_tile(*prompt, wait_group=lambda g: group_copy(g).wait())

    @pl.when((i > 0) & (i < last))
    def _():
        _in_proj_tile(*prompt)

    @pl.when(i == last)
    def _():
        _in_proj_tile(xs_ref, gpre_ref, w_ref, coss_ref, sins_ref, zrs_ref, us_ref, gcs_ref)


def _group_out_shapes(m):
    return [
        jax.ShapeDtypeStruct((m, 4 * COL_TILE), BF16),
        jax.ShapeDtypeStruct((m, COL_TILE), F32),
        jax.ShapeDtypeStruct((m, COL_TILE), BF16),
    ]


def _in_proj(xp, xs, g_pre, w_in, rope_p, rope_s, li, *, bm, pos_blocks):
    m, d = xp.shape
    ms = xs.shape[0]
    n_cols = w_in.shape[1]
    n_tiles = m // bm
    assert n_cols == 7 * COL_TILE and m % bm == 0
    tile = lambda i: jnp.minimum(i, n_tiles - 1)
    rows = lambda width: pl.BlockSpec((bm, width), lambda i: (tile(i), 0))
    pos = pl.BlockSpec((bm, HEAD_DIM), lambda i: (tile(i) % pos_blocks, 0))
    whole = lambda width: _whole_spec((ms, width), 1)
    return pl.pallas_call(
        _in_proj_kernel,
        grid=(n_tiles + 1,),
        in_specs=[rows(d), whole(d), _layer_spec((1, d), li, 1),
                  pl.BlockSpec(memory_space=pl.ANY), pos, pos,
                  whole(HEAD_DIM), whole(HEAD_DIM)],
        out_specs=[rows(4 * COL_TILE), rows(COL_TILE), rows(COL_TILE),
                   whole(4 * COL_TILE), whole(COL_TILE), whole(COL_TILE)],
        out_shape=_group_out_shapes(m) + _group_out_shapes(ms),
        scratch_shapes=[pltpu.VMEM((d, n_cols), BF16),
                        pltpu.SemaphoreType.DMA((n_cols // COL_TILE,))],
        compiler_params=_params(("arbitrary",)),
        name="in_proj",
    )(xp, xs, g_pre, w_in, *rope_p, *rope_s)


def _group_norm_gate(o, g_ret, gate):
    mu = jnp.mean(o, axis=-1, keepdims=True)
    d = o - mu
    var = jnp.mean(d * d, axis=-1, keepdims=True)
    return (d * lax.rsqrt(var + EPS)) * g_ret * gate


def _retention_kernel(q_ref, k_ref, v_ref, gr_ref, gret_ref, lg_ref, *rest, n_cast):
    cast_in, rest = rest[:n_cast], rest[n_cast:]
    o_ref, s_out_ref = rest[:2]
    cast_out = rest[2:2 + n_cast]
    s_ref, decay_ref, cross_ref, kdec_ref = rest[2 + n_cast:]
    for src, dst in zip(cast_in, cast_out):
        dst[...] = src[...].astype(BF16)

    c = RET_CHUNK
    b = pl.program_id(0)
    t = pl.program_id(1)

    @pl.when((b == 0) & (t == 0))
    def _():
        row = lax.broadcasted_iota(jnp.int32, (c, c), 0)
        col = lax.broadcasted_iota(jnp.int32, (c, c), 1)
        rel = (row - col).astype(F32)
        idx = lax.broadcasted_iota(jnp.int32, (c, LANES), 0).astype(F32)
        for h in range(RET_HEADS):
            lg = lg_ref[h]
            decay_ref[h] = jnp.where(rel >= 0, jnp.exp(lg * jnp.maximum(rel, 0.0)), 0.0)
            cross_ref[h] = jnp.exp(lg[:, :LANES] * (idx + 1.0))
            kdec_ref[h] = jnp.exp(lg[:, :LANES] * (c - 1.0 - idx))

    @pl.when(t == 0)
    def _():
        s_ref[...] = jnp.zeros(s_ref.shape, F32)

    for ci in range(q_ref.shape[0] // c):
        rows = slice(ci * c, (ci + 1) * c)
        for h in range(RET_HEADS):
            cols = slice(h * HEAD_DIM, (h + 1) * HEAD_DIM)
            q = q_ref[rows, cols]
            k = k_ref[rows, cols]
            v = v_ref[rows, cols]
            s = s_ref[h]
            g_chunk = jnp.exp(lg_ref[h][:, :LANES] * float(c))
            scores = lax.dot_general(q, k, (((1,), (1,)), ((), ())),
                                     preferred_element_type=F32) * decay_ref[h]
            o = jnp.dot(scores.astype(BF16), v, preferred_element_type=F32)
            o = o + jnp.dot(q, s.astype(BF16), preferred_element_type=F32) * cross_ref[h]
            kd = (k.astype(F32) * kdec_ref[h]).astype(BF16)
            s_ref[h] = g_chunk * s + lax.dot_general(
                kd, v, (((0,), (0,)), ((), ())), preferred_element_type=F32)
            o_ref[rows, cols] = _group_norm_gate(
                o, gret_ref[h], gr_ref[rows, cols].astype(F32)).astype(BF16)

    @pl.when(t == pl.num_programs(1) - 1)
    def _():
        s_out_ref[...] = s_ref[...]


def _retention(zr, g_ret, lg_tab, cast_w, s_prev, li, depth, *, batch, seq, tq):
    m = zr.shape[0]
    nt = seq // tq
    blk = lambda g: pl.BlockSpec((tq, RET_WIDTH), lambda b, t: (b * nt + t, g))
    casts = [_cast_specs(w, li, batch * nt, lambda b, t: b * nt + t) for w in cast_w]
    in_arrays = [zr, zr, zr, zr, g_ret, lg_tab, *cast_w]
    in_specs = [blk(0), blk(1), blk(2), blk(3),
                _layer_spec((RET_HEADS, 1, HEAD_DIM), li, 2),
                pl.BlockSpec(lg_tab.shape, lambda b, t: (0, 0, 0))] + [c[0] for c in casts]
    body, extra, extra_specs, aliases = _alias_args(
        functools.partial(_retention_kernel, n_cast=len(cast_w)),
        len(in_arrays), s_prev, 1)
    return pl.pallas_call(
        body,
        grid=(batch, nt),
        in_specs=in_specs + extra_specs,
        out_specs=[
            pl.BlockSpec((tq, RET_WIDTH), lambda b, t: (b * nt + t, 0)),
            pl.BlockSpec((None, None, RET_HEADS, HEAD_DIM, HEAD_DIM),
                         lambda b, t: (li, b, 0, 0, 0)),
        ] + [c[1] for c in casts],
        out_shape=[
            jax.ShapeDtypeStruct((m, RET_WIDTH), BF16),
            jax.ShapeDtypeStruct((depth, batch, RET_HEADS, HEAD_DIM, HEAD_DIM), F32),
        ] + [c[2] for c in casts],
        scratch_shapes=[
            pltpu.VMEM((RET_HEADS, HEAD_DIM, HEAD_DIM), F32),
            pltpu.VMEM((RET_HEADS, RET_CHUNK, RET_CHUNK), F32),
            pltpu.VMEM((RET_HEADS, RET_CHUNK, LANES), F32),
            pltpu.VMEM((RET_HEADS, RET_CHUNK, LANES), F32),
        ],
        input_output_aliases=aliases,
        compiler_params=_params(("arbitrary", "arbitrary")),
        name="retention",
    )(*in_arrays, *extra)


def _retention_step_kernel(zr_ref, s_ref, gret_ref, gam_ref, o_ref, s_out_ref, oc_ref,
                           *, bb):
    i = pl.program_id(0)
    n_b = zr_ref.shape[0]
    rows = pl.ds(pl.multiple_of(i * bb, bb), bb)
    shift = (n_b - i * bb) % n_b
    for h in range(RET_HEADS):
        hs = lambda g: slice((g * RET_HEADS + h) * HEAD_DIM,
                             (g * RET_HEADS + h + 1) * HEAD_DIM)
        q_t = pltpu.roll(zr_ref[:, hs(0)].astype(F32).T, shift, axis=1)
        k_t = pltpu.roll(zr_ref[:, hs(1)].astype(F32).T, shift, axis=1)
        q = zr_ref[rows, hs(0)].astype(F32)
        k = zr_ref[rows, hs(1)].astype(F32)
        v = zr_ref[rows, hs(2)].astype(F32)
        gate = zr_ref[rows, hs(3)].astype(F32)
        gam = gam_ref[h]
        for b in range(bb):
            s = s_ref[b, h]
            oc_ref[b:b + 1, :] = jnp.sum(q_t[:, b:b + 1] * s, axis=0, keepdims=True)
            s_out_ref[b, h] = gam * s + k_t[:, b:b + 1] * v[b:b + 1, :]
        o = jnp.sum(q * k, axis=-1, keepdims=True) * v + oc_ref[...] * gam
        o_ref[:, hs(0)] = _group_norm_gate(o, gret_ref[h], gate).astype(BF16)


def _retention_step(zr, state, g_ret, gam_tab, s_prev, li, *, bb):
    n_b = zr.shape[0]
    st_spec = pl.BlockSpec((None, bb, RET_HEADS, HEAD_DIM, HEAD_DIM),
                           lambda i: (li, i, 0, 0, 0))
    in_arrays = [zr, state, g_ret, gam_tab]
    in_specs = [pl.BlockSpec(zr.shape, lambda i: (0, 0)), st_spec,
                _layer_spec((RET_HEADS, 1, HEAD_DIM), li, 1),
                pl.BlockSpec(gam_tab.shape, lambda i: (0, 0, 0))]
    body, extra, extra_specs, aliases = _alias_args(
        functools.partial(_retention_step_kernel, bb=bb), len(in_arrays), s_prev, 1)
    return pl.pallas_call(
        body,
        grid=(n_b // bb,),
        in_specs=in_specs + extra_specs,
        out_specs=[pl.BlockSpec((bb, RET_WIDTH), lambda i: (i, 0)), st_spec],
        out_shape=[
            jax.ShapeDtypeStruct((n_b, RET_WIDTH), BF16),
            jax.ShapeDtypeStruct(state.shape, F32),
        ],
        scratch_shapes=[pltpu.VMEM((bb, HEAD_DIM), F32)],
        input_output_aliases=aliases,
        compiler_params=_params(("arbitrary",)),
        name="retention_step",
    )(*in_arrays, *extra)


def _conv_tail(c, gcn_ref, bcn_ref, wpw_ref, bpw_ref, gate):
    mu = jnp.mean(c, axis=-1, keepdims=True)
    d = c - mu
    var = jnp.mean(d * d, axis=-1, keepdims=True)
    y = _silu(d * lax.rsqrt(var + EPS) * gcn_ref[...] + bcn_ref[...])
    y = jnp.dot(y.astype(BF16), wpw_ref[...], preferred_element_type=F32) + bpw_ref[...]
    return (y * gate).astype(BF16)


def _conv_kernel(u_ref, gc_ref, wdw_ref, bdw_ref, gcn_ref, bcn_ref, wpw_ref, bpw_ref,
                 *rest, n_cast):
    cast_in, rest = rest[:n_cast], rest[n_cast:]
    o_ref, st_ref = rest[:2]
    cast_out = rest[2:2 + n_cast]
    full_ref, c_ref, sh_ref = rest[2 + n_cast:]
    for src, dst in zip(cast_in, cast_out):
        dst[...] = src[...].astype(BF16)

    t = pl.program_id(1)
    tt, ch = u_ref.shape
    pad = HIST_ROWS - (CONV_K - 1)

    @pl.when(t == 0)
    def _():
        full_ref[0:HIST_ROWS, :] = jnp.zeros((HIST_ROWS, ch), F32)

    full_ref[HIST_ROWS:HIST_ROWS + tt, :] = u_ref[...]

    for p in range(SUBLANES):
        span = tt + SUBLANES * ((CONV_K - 1 - p) // SUBLANES)
        sh_ref[p, 0:span, :] = full_ref[pad + p:pad + p + span, :]

    for r0 in range(0, tt, CONV_ROW_BLOCK):
        for cb in range(ch // LANES):
            cs = slice(cb * LANES, (cb + 1) * LANES)
            acc = bdw_ref[:, cs]
            for j in range(CONV_K):
                p, a = j % SUBLANES, j // SUBLANES
                row = r0 + a * SUBLANES
                acc = acc + sh_ref[p, row:row + CONV_ROW_BLOCK, cs] * wdw_ref[j:j + 1, cs]
            c_ref[r0:r0 + CONV_ROW_BLOCK, cs] = acc

    o_ref[...] = _conv_tail(c_ref[...], gcn_ref, bcn_ref, wpw_ref, bpw_ref,
                            gc_ref[...].astype(F32))

    @pl.when(t == pl.num_programs(1) - 1)
    def _():
        st_ref[...] = full_ref[tt + pad:tt + HIST_ROWS, :]

    full_ref[0:HIST_ROWS, :] = full_ref[tt:tt + HIST_ROWS, :]


def _conv_weight_specs(ch, li, n_grid):
    vec = _layer_spec((1, ch), li, n_grid)
    return [_layer_spec((CONV_K, ch), li, n_grid), vec, vec, vec,
            _whole_spec((ch, ch), n_grid), vec]


def _conv(u, gc, conv_w, cast_w, cast_li, st_prev, li, depth, *, batch, seq, tt):
    m, ch = u.shape
    nt = seq // tt
    tile = pl.BlockSpec((tt, ch), lambda b, t: (b * nt + t, 0))
    casts = [_cast_specs(w, cast_li, batch * nt, lambda b, t: b * nt + t) for w in cast_w]
    in_arrays = [u, gc, *conv_w, *cast_w]
    in_specs = [tile, tile] + _conv_weight_specs(ch, li, 2) + [c[0] for c in casts]
    body, extra, extra_specs, aliases = _alias_args(
        functools.partial(_conv_kernel, n_cast=len(cast_w)), len(in_arrays), st_prev, 1)
    return pl.pallas_call(
        body,
        grid=(batch, nt),
        in_specs=in_specs + extra_specs,
        out_specs=[
            tile,
            pl.BlockSpec((None, None, CONV_K - 1, ch), lambda b, t: (li, b, 0, 0)),
        ] + [c[1] for c in casts],
        out_shape=[
            jax.ShapeDtypeStruct((m, ch), BF16),
            jax.ShapeDtypeStruct((depth, batch, CONV_K - 1, ch), F32),
        ] + [c[2] for c in casts],
        scratch_shapes=[
            pltpu.VMEM((HIST_ROWS + tt, ch), F32),
            pltpu.VMEM((tt, ch), F32),
            pltpu.VMEM((SUBLANES, tt + HIST_ROWS - SUBLANES, ch), F32),
        ],
        input_output_aliases=aliases,
        compiler_params=_params(("arbitrary", "arbitrary")),
        name="conv",
    )(*in_arrays, *extra)


def _conv_step_kernel(u_ref, gc_ref, st_ref, wdw_ref, bdw_ref, gcn_ref, bcn_ref, wpw_ref,
                      bpw_ref, o_ref, st_out_ref):
    hist = CONV_K - 1
    u = u_ref[...]
    c = u * wdw_ref[hist:hist + 1, :] + bdw_ref[...]
    for j in range(hist):
        c = c + st_ref[j] * wdw_ref[j:j + 1, :]
    o_ref[...] = _conv_tail(c, gcn_ref, bcn_ref, wpw_ref, bpw_ref,
                            gc_ref[...].astype(F32))
    for j in range(hist - 1):
        st_out_ref[j] = st_ref[j + 1]
    st_out_ref[hist - 1] = u


def _conv_step(u, gc, state, conv_w, st_prev, li, *, bb):
    n_b, ch = u.shape
    st_spec = pl.BlockSpec((None, CONV_K - 1, bb, ch), lambda i: (li, 0, i, 0))
    tile = pl.BlockSpec((bb, ch), lambda i: (i, 0))
    in_arrays = [u, gc, state, *conv_w]
    in_specs = [tile, tile, st_spec] + _conv_weight_specs(ch, li, 1)
    body, extra, extra_specs, aliases = _alias_args(
        _conv_step_kernel, len(in_arrays), st_prev, 1)
    return pl.pallas_call(
        body,
        grid=(n_b // bb,),
        in_specs=in_specs + extra_specs,
        out_specs=[tile, st_spec],
        out_shape=[
            jax.ShapeDtypeStruct((n_b, ch), BF16),
            jax.ShapeDtypeStruct(state.shape, F32),
        ],
        input_output_aliases=aliases,
        compiler_params=_params(("arbitrary",)),
        name="conv_step",
    )(*in_arrays, *extra)


def _out_tile(ret_ref, conv_ref, x_ref, p_ref, wo_ref, wg_ref, wp_ref, gpost_ref,
              gple_ref, y_ref, wait_weight=None):
    wait = wait_weight if wait_weight is not None else (lambda k: None)
    rw = ret_ref.shape[1]
    wait(0)
    m = (jnp.dot(ret_ref[...], wo_ref[0:rw, :], preferred_element_type=F32)
         + jnp.dot(conv_ref[...], wo_ref[rw:, :], preferred_element_type=F32))
    h = x_ref[...] + _rms(m, gpost_ref[...])
    h_b = h.astype(BF16)
    wait(2)
    e = jnp.dot(p_ref[...].astype(BF16), wp_ref[...], preferred_element_type=F32)
    e = _rms(e, gple_ref[...])
    wait(1)
    d = h.shape[1]
    for c0 in range(0, d, GATE_CHUNK):
        cols = slice(c0, c0 + GATE_CHUNK)
        gate = jax.nn.sigmoid(jnp.dot(h_b, wg_ref[:, cols], preferred_element_type=F32))
        y_ref[:, cols] = h[:, cols] + gate * e[:, cols]


def _out_kernel(retp_ref, convp_ref, xp_ref, pp_ref, rets_ref, convs_ref, xs_ref, ps_ref,
                wo_hbm_ref, wg_hbm_ref, wp_hbm_ref, gpost_ref, gple_ref, yp_ref, ys_ref,
                wo_ref, wg_ref, wp_ref, w_sem):
    i = pl.program_id(0)
    last = pl.num_programs(0) - 1
    weights = (wo_ref, wg_ref, wp_ref, gpost_ref, gple_ref)

    def weight_copy(k):
        src, dst = ((wo_hbm_ref, wo_ref), (wg_hbm_ref, wg_ref), (wp_hbm_ref, wp_ref))[k]
        return pltpu.make_async_copy(src, dst, w_sem.at[k])

    @pl.when(i == 0)
    def _():
        for k in (0, 2, 1):
            weight_copy(k).start()
        _out_tile(retp_ref, convp_ref, xp_ref, pp_ref, *weights, yp_ref,
                  wait_weight=lambda k: weight_copy(k).wait())

    @pl.when((i > 0) & (i < last))
    def _():
        _out_tile(retp_ref, convp_ref, xp_ref, pp_ref, *weights, yp_ref)

    @pl.when(i == last)
    def _():
        _out_tile(rets_ref, convs_ref, xs_ref, ps_ref, *weights, ys_ref)


def _out_proj(prompt, sample, w_out, w_gate, w_ple, g_post, g_ple, li, *, bm):
    ret_p, conv_p, xp, pp = prompt
    ret_s, conv_s, xs, ps = sample
    m, d = xp.shape
    ms = xs.shape[0]
    pd = pp.shape[2]
    n_tiles = m // bm
    tile = lambda i: jnp.minimum(i, n_tiles - 1)
    rows = lambda width: pl.BlockSpec((bm, width), lambda i: (tile(i), 0))
    whole = lambda width: _whole_spec((ms, width), 1)
    return pl.pallas_call(
        _out_kernel,
        grid=(n_tiles + 1,),
        in_specs=[
            rows(ret_p.shape[1]), rows(conv_p.shape[1]), rows(d),
            pl.BlockSpec((None, bm, pd), lambda i: (li, tile(i), 0)),
            whole(ret_s.shape[1]), whole(conv_s.shape[1]), whole(d),
            pl.BlockSpec((None, ms, pd), lambda i: (li, 0, 0)),
            pl.BlockSpec(memory_space=pl.ANY), pl.BlockSpec(memory_space=pl.ANY),
            pl.BlockSpec(memory_space=pl.ANY),
            _layer_spec((1, d), li, 1), _layer_spec((1, d), li, 1),
        ],
        out_specs=[rows(d), whole(d)],
        out_shape=[jax.ShapeDtypeStruct((m, d), F32), jax.ShapeDtypeStruct((ms, d), F32)],
        scratch_shapes=[pltpu.VMEM(w_out.shape, BF16), pltpu.VMEM(w_gate.shape, BF16),
                        pltpu.VMEM(w_ple.shape, BF16), pltpu.SemaphoreType.DMA((3,))],
        compiler_params=_params(("arbitrary",)),
        name="out_proj",
    )(ret_p, conv_p, xp, pp, ret_s, conv_s, xs, ps, w_out, w_gate, w_ple, g_post, g_ple)


def _rope_tables(pos):
    half = HEAD_DIM // 2
    inv = ROPE_BASE ** (-jnp.arange(half, dtype=F32) / half)
    ang = pos.astype(F32)[:, None] * inv[None, :]
    cos, sin = jnp.cos(ang), jnp.sin(ang)
    return jnp.concatenate([cos, cos], axis=-1), jnp.concatenate([-sin, sin], axis=-1)


def kernel(x_prompt, x_sample, state_ret, state_conv, p_prompt, p_sample, w_in, w_out,
           g_ret, w_dw, b_dw, g_cn, b_cn, w_pw2, b_pw2, g_pre, g_post, w_ple, g_ple,
           w_ple_gate):
    depth = w_in.shape[0]
    bp, lp, d = x_prompt.shape
    bs, ls, _ = x_sample.shape
    assert ls == 1 and lp % RET_CHUNK == 0

    w_in_b = w_in[0].astype(BF16)

    log_g = jnp.log1p(-jnp.exp2(-5.0 - jnp.arange(RET_HEADS, dtype=F32)))
    lg_tab = jnp.broadcast_to(log_g[:, None, None], (RET_HEADS, 1, RET_CHUNK))
    gam_tab = jnp.broadcast_to(jnp.exp(log_g)[:, None, None], (RET_HEADS, 1, HEAD_DIM))
    rope_p = _rope_tables(jnp.arange(lp, dtype=jnp.int32))
    rope_s = _rope_tables(jnp.full((bs,), PAST_LEN, dtype=jnp.int32))

    vec = lambda a: a.reshape(depth, 1, -1)
    g_pre3, g_post3, g_ple3 = vec(g_pre), vec(g_post), vec(g_ple)
    g_ret4 = g_ret.reshape(depth, RET_HEADS, 1, HEAD_DIM)
    conv_vecs = (vec(b_dw), vec(g_cn), vec(b_cn))
    b_pw3 = vec(b_pw2)
    state_conv_t = jnp.transpose(state_conv, (0, 2, 1, 3))
    pp = p_prompt.reshape(depth, bp * lp, -1)
    ps = p_sample.reshape(depth, bs, -1)

    bm_in, bm_out, tq, tt = 256, 512, 512, 512
    xp = x_prompt.reshape(bp * lp, d)
    xs = x_sample.reshape(bs, d)
    ret_p = conv_p = ret_s = conv_s = None
    for li in range(depth):
        (zr_p, u_p, gc_p, zr_s, u_s, gc_s) = _in_proj(
            xp, xs, g_pre3, w_in_b, rope_p, rope_s, li, bm=bm_in, pos_blocks=lp // bm_in)

        if li == 0:
            r_p, ret_p, w_ple_b, w_out_b, w_gate_b, w_pw_b = _retention(
                zr_p, g_ret4, lg_tab, (w_ple, w_out, w_ple_gate, w_pw2), ret_p, li, depth,
                batch=bp, seq=lp, tq=tq)
        else:
            r_p, ret_p, w_ple_b = _retention(zr_p, g_ret4, lg_tab, (w_ple,), ret_p, li,
                                             depth, batch=bp, seq=lp, tq=tq)
        conv_w = (w_dw, *conv_vecs, w_pw_b, b_pw3)
        out_w = (w_out_b, w_gate_b, w_ple_b)
        next_w = (w_in, w_out, w_ple_gate, w_pw2) if li + 1 < depth else ()
        c_p, conv_p, *next_b = _conv(u_p, gc_p, conv_w, next_w, li + 1, conv_p, li, depth,
                                     batch=bp, seq=lp, tt=tt)
        if next_b:
            w_in_b, w_out_b, w_gate_b, w_pw_b = next_b

        r_s, ret_s = _retention_step(zr_s, state_ret, g_ret4, gam_tab, ret_s, li, bb=16)
        c_s, conv_s = _conv_step(u_s, gc_s, state_conv_t, conv_w, conv_s, li, bb=16)

        xp, xs = _out_proj((r_p, c_p, xp, pp), (r_s, c_s, xs, ps),
                           *out_w, g_post3, g_ple3, li, bm=bm_out)

    return (xp.reshape(bp, lp, d), xs.reshape(bs, ls, d), ret_p, conv_p, ret_s,
            jnp.transpose(conv_s, (0, 2, 1, 3)))
```
